```python
import math
import jax
import jax.numpy as jnp
from jax import lax
import numpy as np

D_MODEL = 4096
BATCH = 32
SEQ = 256
DEPTH = 4
DEC_BATCH = 4
DEC_SEQ = 1024
PAST_LEN = 512

GRID_W = 64
HEAD_DIM = 128
N_BRANCHES = 4
BRANCH_WIDTH = D_MODEL // N_BRANCHES
Q_BLOCK = 128
ROPE_THETA = 10000.0
NEG_INF = -1e30
EPS = 1e-6

NA_HEADS = BRANCH_WIDTH // HEAD_DIM
NA_WIN_H = 8
NA_WIN_W = 16

MLA_HEADS = BRANCH_WIDTH // HEAD_DIM
MLA_NOPE = HEAD_DIM
MLA_ROPE = HEAD_DIM // 2
MLA_V = HEAD_DIM
MLA_Q_LORA = D_MODEL // 4
MLA_KV_LORA = D_MODEL // 8

GQA_Q_HEADS = BRANCH_WIDTH // HEAD_DIM
GQA_KV_HEADS = 2
GQA_GROUP = GQA_Q_HEADS // GQA_KV_HEADS

SSM_WIDTH = BRANCH_WIDTH
SSM_GROUP_CH = 16
SSM_GROUPS = SSM_WIDTH // SSM_GROUP_CH
SSM_STATE = 64

N_EXPERTS = 32
TOP_K = 4
D_FF = D_MODEL // 4
SWIGLU_LIMIT = 7.0
SWIGLU_ALPHA = 1.702
MOE_BLOCK = 128

IN_WIDTHS = (NA_HEADS * HEAD_DIM, NA_HEADS * HEAD_DIM, NA_HEADS * HEAD_DIM,
             MLA_Q_LORA, MLA_KV_LORA, MLA_ROPE,
             GQA_Q_HEADS * HEAD_DIM, GQA_KV_HEADS * HEAD_DIM, GQA_KV_HEADS * HEAD_DIM,
             SSM_WIDTH, N_BRANCHES * D_MODEL)
IN_WIDTH = sum(IN_WIDTHS)
IN_SPLITS = tuple(int(v) for v in np.cumsum(IN_WIDTHS)[:-1])

F32 = jnp.float32

kernel_name = 'hybrid_diffusion_trunk_step'


def rmsnorm(x, g):
    x32 = x.astype(F32)
    y = x32 * lax.rsqrt(jnp.mean(x32 * x32, axis=-1, keepdims=True) + EPS)
    return (y * g.astype(F32)).astype(x.dtype)


def axial_rope_tables(n_tok, rot_dim):
    t = jnp.arange(n_tok)
    n_freq = rot_dim // 4
    freqs = ROPE_THETA ** (-jnp.arange(n_freq, dtype=F32) / n_freq)
    ang = jnp.concatenate([(t // GRID_W).astype(F32)[:, None] * freqs,
                           (t % GRID_W).astype(F32)[:, None] * freqs], axis=-1)
    return jnp.cos(ang), jnp.sin(ang)


def apply_rope(x, cos, sin):
    shape = (1, x.shape[1]) + (1,) * (x.ndim - 3) + (cos.shape[-1],)
    cos = cos.reshape(shape)
    sin = sin.reshape(shape)
    x32 = x.astype(F32)
    half = x.shape[-1] // 2
    x1, x2 = x32[..., :half], x32[..., half:]
    return jnp.concatenate([x1 * cos - x2 * sin, x1 * sin + x2 * cos], axis=-1).astype(x.dtype)


def blocked_attention(q, k, v):
    bsz, s, hk, g, dq = q.shape
    scale = dq ** -0.5
    nb = s // Q_BLOCK
    qb = jnp.moveaxis(q.reshape(bsz, nb, Q_BLOCK, hk, g, dq), 1, 0)

    def one_block(qi):
        sc = jnp.einsum('bqhgd,bthd->bhgqt', qi, k, preferred_element_type=F32) * scale
        p = jax.nn.softmax(sc, axis=-1).astype(v.dtype)
        return jnp.einsum('bhgqt,bthd->bqhgd', p, v)

    ob = lax.map(one_block, qb)
    return jnp.moveaxis(ob, 0, 1).reshape(bsz, s, hk, g, v.shape[-1])


def na_latent_attention(q, k, v, k_ctx, v_ctx, rpb):
    bsz, s, nh, dh = q.shape
    rows = s // GRID_W
    wh = min(NA_WIN_H, rows)
    r = jnp.arange(rows)
    key_rows = jnp.clip(r - wh // 2, 0, rows - wh)[:, None] + jnp.arange(wh)[None, :]
    col = jnp.arange(GRID_W)
    col_start = jnp.clip(col - NA_WIN_W // 2, 0, GRID_W - NA_WIN_W)
    in_win = (col[None, :] >= col_start[:, None]) & (col[None, :] < col_start[:, None] + NA_WIN_W)
    d_row = key_rows - r[:, None] + (NA_WIN_H - 1)
    d_col = jnp.clip(col[None, :] - col[:, None], -(NA_WIN_W - 1), NA_WIN_W - 1) + (NA_WIN_W - 1)
    bias = rpb.astype(F32)[:, d_row[:, None, :, None], d_col[None, :, None, :]]
    qg = q.reshape(bsz, rows, GRID_W, nh, dh)
    kg = k.reshape(bsz, rows, GRID_W, nh, dh)[:, key_rows]
    vg = v.reshape(bsz, rows, GRID_W, nh, dh)[:, key_rows]
    scale = dh ** -0.5
    s_loc = jnp.einsum('brqhd,brwkhd->bhrqwk', qg, kg, preferred_element_type=F32) * scale + bias[None]
    s_loc = jnp.where(in_win[:, None, :], s_loc, NEG_INF)
    s_ctx = jnp.einsum('brqhd,blhd->bhrql', qg, k_ctx, preferred_element_type=F32) * scale
    n_loc = wh * GRID_W
    probs = jax.nn.softmax(jnp.concatenate([s_loc.reshape(bsz, nh, rows, GRID_W, n_loc), s_ctx], axis=-1),
                           axis=-1).astype(v.dtype)
    p_loc = probs[..., :n_loc].reshape(bsz, nh, rows, GRID_W, wh, GRID_W)
    p_ctx = probs[..., n_loc:]
    out = jnp.einsum('bhrqwk,brwkhd->brqhd', p_loc, vg) + jnp.einsum('bhrql,blhd->brqhd', p_ctx, v_ctx)
    return out.reshape(bsz, s, nh * dh)


def mla_expand(ckv, w_kv_up, k_gain):
    bsz, t, _ = ckv.shape
    kv = (ckv @ w_kv_up).reshape(bsz, t, MLA_HEADS, MLA_NOPE + MLA_V)
    return rmsnorm(kv[..., :MLA_NOPE], k_gain), kv[..., MLA_NOPE:]


def _ssm_combine(left, right):
    a_l, b_l = left
    a_r, b_r = right
    return a_l * a_r, a_r * b_l + b_r


def s5_bidirectional(u, p, h0):
    bsz, s, _ = u.shape
    lam = lax.complex(p['s5_lambda_re'].astype(F32), p['s5_lambda_im'].astype(F32))
    step = jnp.exp(p['s5_log_step'].astype(F32))[..., None]
    lam_bar = jnp.exp(lam * step)
    b_mat = lax.complex(p['s5_b_re'].astype(F32), p['s5_b_im'].astype(F32))
    c_mat = lax.complex(p['s5_c_re'].astype(F32), p['s5_c_im'].astype(F32))
    b_bar = ((lam_bar - 1.0) / lam)[..., None] * b_mat
    u32 = u.astype(F32)
    ug = u32.reshape(bsz, s, SSM_GROUPS, SSM_GROUP_CH).astype(jnp.complex64)
    bu = jnp.einsum('bsgc,egnc->bsegn', ug, b_bar)
    bu = jnp.concatenate([bu[:, :, :1], jnp.flip(bu[:, :, 1:], axis=1)], axis=2)
    a = jnp.broadcast_to(lam_bar[None, None], (1, s) + lam_bar.shape)
    a_cum, h = lax.associative_scan(_ssm_combine, (a, bu), axis=1)
    h = h + a_cum * h0[:, None]
    h_final = h[:, -1]
    h = jnp.concatenate([h[:, :, :1], jnp.flip(h[:, :, 1:], axis=1)], axis=2)
    y = jnp.einsum('bsegn,egcn->bsgc', h, c_mat).real.reshape(bsz, s, SSM_WIDTH)
    y = jax.nn.gelu(y + p['s5_d'].astype(F32) * u32).astype(u.dtype)
    y = y * jax.nn.sigmoid(y @ p['s5_w_glu'])
    return y, jnp.real(h_final).astype(u.dtype), jnp.imag(h_final).astype(u.dtype)


def moe_ffn(h, p):
    n_tok = h.shape[0]
    logits = jnp.dot(h, p['moe_w_router'], preferred_element_type=F32) + p['moe_b_router'].astype(F32)
    top_logit, top_e = lax.top_k(logits, TOP_K)
    top_w = jax.nn.softmax(top_logit, axis=-1)
    n_assign = n_tok * TOP_K
    flat_e = top_e.reshape(-1)
    order = jnp.argsort(flat_e)
    sorted_e = flat_e[order]
    counts = jnp.bincount(flat_e, length=N_EXPERTS)
    padded = (counts + MOE_BLOCK - 1) // MOE_BLOCK * MOE_BLOCK
    start = jnp.cumsum(counts) - counts
    pad_end = jnp.cumsum(padded)
    pad_start = pad_end - padded
    slot = pad_start[sorted_e] + jnp.arange(n_assign) - start[sorted_e]
    n_blocks = -(-n_assign // MOE_BLOCK) + N_EXPERTS
    n_slots = n_blocks * MOE_BLOCK
    slot_tok = jnp.zeros((n_slots,), jnp.int32).at[slot].set((order // TOP_K).astype(jnp.int32))
    slot_w = jnp.zeros((n_slots,), F32).at[slot].set(top_w.reshape(-1)[order])
    block_e = jnp.minimum(jnp.searchsorted(pad_end, jnp.arange(n_blocks) * MOE_BLOCK, side='right'),
                          N_EXPERTS - 1)
    xs = h[slot_tok].reshape(n_blocks, MOE_BLOCK, h.shape[-1])

    def expert_block(args):
        e, xb = args
        gate = jnp.minimum(xb @ p['moe_w_gate'][e] + p['moe_b_gate'][e], SWIGLU_LIMIT)
        up = jnp.clip(xb @ p['moe_w_up'][e] + p['moe_b_up'][e], -SWIGLU_LIMIT, SWIGLU_LIMIT)
        act = gate * jax.nn.sigmoid(SWIGLU_ALPHA * gate) * (up + 1.0)
        return act @ p['moe_w_down'][e] + p['moe_b_down'][e]

    ys = lax.map(expert_block, (block_e, xs))
    ys = ys.reshape(n_slots, -1) * slot_w[:, None]
    return jax.ops.segment_sum(ys, slot_tok, num_segments=n_tok).astype(h.dtype)


def mixer_block(h, p, ctx):
    bsz, s, _ = h.shape
    latent = ctx is not None
    z = h @ p['w_in']
    a_q, a_k, a_v, b_ql, b_kvl, b_kr, c_q, c_k, c_v, d_u, gate_in = jnp.split(z, IN_SPLITS, axis=-1)

    na_q = rmsnorm(a_q.reshape(bsz, s, NA_HEADS, HEAD_DIM), p['na_qk_g'][0])
    na_k = rmsnorm(a_k.reshape(bsz, s, NA_HEADS, HEAD_DIM), p['na_qk_g'][1])
    na_v = a_v.reshape(bsz, s, NA_HEADS, HEAD_DIM)
    if latent:
        o_a = na_latent_attention(na_q, na_k, na_v, ctx['na_k'], ctx['na_v'], p['na_rpb'])
    else:
        o_a = blocked_attention(na_q[:, :, :, None], na_k, na_v)
    o_a = o_a.reshape(bsz, s, BRANCH_WIDTH)

    mq = (rmsnorm(b_ql, p['mla_q_lora_g']) @ p['mla_w_q_up']).reshape(bsz, s, MLA_HEADS, MLA_NOPE + MLA_ROPE)
    mq_nope = rmsnorm(mq[..., :MLA_NOPE], p['mla_nope_g'][0])
    mq_rope = rmsnorm(mq[..., MLA_NOPE:], p['mla_rope_g'][0])
    ckv = rmsnorm(b_kvl, p['mla_kv_lora_g'])
    krope = rmsnorm(b_kr, p['mla_rope_g'][1])
    mk_nope, mv = mla_expand(ckv, p['mla_w_kv_up'], p['mla_nope_g'][1])
    if latent:
        cos_m, sin_m = axial_rope_tables(s, MLA_ROPE)
        mq_rope = apply_rope(mq_rope, cos_m, sin_m)
        ck_nope, cv = mla_expand(ctx['mla_ckv'], p['mla_w_kv_up'], p['mla_nope_g'][1])
        k_nope_all = jnp.concatenate([mk_nope, ck_nope], axis=1)
        v_all_b = jnp.concatenate([mv, cv], axis=1)
        krope_all = jnp.concatenate([apply_rope(krope, cos_m, sin_m), ctx['mla_krope']], axis=1)
    else:
        k_nope_all, v_all_b, krope_all = mk_nope, mv, krope
    mk = jnp.concatenate([k_nope_all, jnp.broadcast_to(krope_all[:, :, None, :],
                                                       k_nope_all.shape[:3] + (MLA_ROPE,))], axis=-1)
    mq_full = jnp.concatenate([mq_nope, mq_rope], axis=-1)
    o_b = blocked_attention(mq_full[:, :, :, None], mk, v_all_b).reshape(bsz, s, BRANCH_WIDTH)

    gq = rmsnorm(c_q.reshape(bsz, s, GQA_KV_HEADS, GQA_GROUP, HEAD_DIM), p['gqa_qk_g'][0])
    gk = rmsnorm(c_k.reshape(bsz, s, GQA_KV_HEADS, HEAD_DIM), p['gqa_qk_g'][1])
    gv = c_v.reshape(bsz, s, GQA_KV_HEADS, HEAD_DIM)
    if latent:
        cos_g, sin_g = axial_rope_tables(s, HEAD_DIM)
        gq = apply_rope(gq, cos_g, sin_g)
        k_all_c = jnp.concatenate([apply_rope(gk, cos_g, sin_g), ctx['gqa_k']], axis=1)
        v_all_c = jnp.concatenate([gv, ctx['gqa_v']], axis=1)
    else:
        k_all_c, v_all_c = gk, gv
    o_c = blocked_attention(gq, k_all_c, v_all_c).reshape(bsz, s, BRANCH_WIDTH)

    if latent:
        h0 = lax.complex(ctx['s5_re'].astype(F32), ctx['s5_im'].astype(F32))
    else:
        h0 = jnp.zeros((bsz, 2, SSM_GROUPS, SSM_STATE), jnp.complex64)
    o_d, s5_re, s5_im = s5_bidirectional(d_u, p, h0)

    gates = jax.nn.sigmoid(gate_in.reshape(bsz, s, N_BRANCHES, D_MODEL))
    branches = (o_a, o_b, o_c, o_d)
    merged = gates[:, :, 0] * (branches[0] @ p['w_branch'][0])
    for i in range(1, N_BRANCHES):
        merged = merged + gates[:, :, i] * (branches[i] @ p['w_branch'][i])
    out = merged @ p['w_out']
    if latent:
        return out, None
    return out, (na_k, na_v, ckv, krope, gk, gv, s5_re, s5_im)


def trunk_layer(x, cond, p, ctx):
    mod = jax.nn.silu(cond) @ p['w_ada'] + p['b_ada']
    shift1, scale1, gate1, shift2, scale2, gate2 = jnp.split(mod[:, None, :], 6, axis=-1)
    h = rmsnorm(x, p['norm_g'][0]) * (1.0 + scale1) + shift1
    mix, new_ctx = mixer_block(h, p, ctx)
    x = x + gate1 * mix
    h = rmsnorm(x, p['norm_g'][1]) * (1.0 + scale2) + shift2
    ffn = moe_ffn(h.reshape(-1, D_MODEL), p).reshape(x.shape)
    x = x + gate2 * ffn
    return x, new_ctx


def setup_inputs(seed: int = 0) -> dict:
    key = jax.random.key(seed)
    keys = jax.random.split(key, 48)
    cnt = [0]

    def nk():
        cnt[0] += 1
        return keys[cnt[0] - 1]

    def nrm(shape, scale):
        return jax.random.normal(nk(), shape, F32) * scale

    def gain(shape):
        return 1.0 + 0.02 * jax.random.normal(nk(), shape, F32)

    s5_shape = (DEPTH, 2, SSM_GROUPS, SSM_STATE)
    return {
        'x_prompt': nrm((BATCH, SEQ, D_MODEL), 1.0),
        'x_sample': nrm((DEC_BATCH, DEC_SEQ, D_MODEL), 1.0),
        'cache_na_k': nrm((DEC_BATCH, DEPTH, PAST_LEN, NA_HEADS, HEAD_DIM), 1.0),
        'cache_na_v': nrm((DEC_BATCH, DEPTH, PAST_LEN, NA_HEADS, HEAD_DIM), 1.0),
        'cache_mla_ckv': nrm((DEC_BATCH, DEPTH, PAST_LEN, MLA_KV_LORA), 1.0),
        'cache_mla_krope': nrm((DEC_BATCH, DEPTH, PAST_LEN, MLA_ROPE), 1.0),
        'cache_gqa_k': nrm((DEC_BATCH, DEPTH, PAST_LEN, GQA_KV_HEADS, HEAD_DIM), 1.0),
        'cache_gqa_v': nrm((DEC_BATCH, DEPTH, PAST_LEN, GQA_KV_HEADS, HEAD_DIM), 1.0),
        'state_s5_re': nrm((DEC_BATCH, DEPTH, 2, SSM_GROUPS, SSM_STATE), 0.5),
        'state_s5_im': nrm((DEC_BATCH, DEPTH, 2, SSM_GROUPS, SSM_STATE), 0.5),
        'c': nrm((DEC_BATCH, D_MODEL), 1.0),
        'c_ctx': nrm((D_MODEL,), 1.0),
        'norm_g': gain((DEPTH, 2, D_MODEL)),
        'w_ada': nrm((DEPTH, D_MODEL, 6 * D_MODEL), 0.5 * D_MODEL ** -0.5),
        'b_ada': nrm((DEPTH, 6 * D_MODEL), 0.02),
        'w_in': nrm((DEPTH, D_MODEL, IN_WIDTH), D_MODEL ** -0.5),
        'na_qk_g': gain((DEPTH, 2, HEAD_DIM)),
        'na_rpb': nrm((DEPTH, NA_HEADS, 2 * NA_WIN_H - 1, 2 * NA_WIN_W - 1), 0.1),
        'mla_q_lora_g': gain((DEPTH, MLA_Q_LORA)),
        'mla_kv_lora_g': gain((DEPTH, MLA_KV_LORA)),
        'mla_w_q_up': nrm((DEPTH, MLA_Q_LORA, MLA_HEADS * (MLA_NOPE + MLA_ROPE)), MLA_Q_LORA ** -0.5),
        'mla_w_kv_up': nrm((DEPTH, MLA_KV_LORA, MLA_HEADS * (MLA_NOPE + MLA_V)), MLA_KV_LORA ** -0.5),
        'mla_nope_g': gain((DEPTH, 2, MLA_NOPE)),
        'mla_rope_g': gain((DEPTH, 2, MLA_ROPE)),
        'gqa_qk_g': gain((DEPTH, 2, HEAD_DIM)),
        's5_lambda_re': -0.5 + nrm(s5_shape, 0.01),
        's5_lambda_im': jnp.pi * jnp.arange(SSM_STATE, dtype=F32) + nrm(s5_shape, 0.01),
        's5_log_step': jax.random.uniform(nk(), (DEPTH, 2, SSM_GROUPS), F32, math.log(1e-3), math.log(1e-1)),
        's5_b_re': nrm((DEPTH, 2, SSM_GROUPS, SSM_STATE, SSM_GROUP_CH), (2 * SSM_GROUP_CH) ** -0.5),
        's5_b_im': nrm((DEPTH, 2, SSM_GROUPS, SSM_STATE, SSM_GROUP_CH), (2 * SSM_GROUP_CH) ** -0.5),
        's5_c_re': nrm((DEPTH, 2, SSM_GROUPS, SSM_GROUP_CH, SSM_STATE), (2 * SSM_STATE) ** -0.5),
        's5_c_im': nrm((DEPTH, 2, SSM_GROUPS, SSM_GROUP_CH, SSM_STATE), (2 * SSM_STATE) ** -0.5),
        's5_d': nrm((DEPTH, SSM_WIDTH), 0.5),
        's5_w_glu': nrm((DEPTH, SSM_WIDTH, SSM_WIDTH), SSM_WIDTH ** -0.5),
        'w_branch': nrm((DEPTH, N_BRANCHES, BRANCH_WIDTH, D_MODEL), BRANCH_WIDTH ** -0.5),
        'w_out': nrm((DEPTH, D_MODEL, D_MODEL), D_MODEL ** -0.5),
        'moe_w_router': nrm((DEPTH, D_MODEL, N_EXPERTS), D_MODEL ** -0.5),
        'moe_b_router': nrm((DEPTH, N_EXPERTS), 0.01),
        'moe_w_gate': nrm((DEPTH, N_EXPERTS, D_MODEL, D_FF), D_MODEL ** -0.5),
        'moe_b_gate': nrm((DEPTH, N_EXPERTS, D_FF), 0.02),
        'moe_w_up': nrm((DEPTH, N_EXPERTS, D_MODEL, D_FF), D_MODEL ** -0.5),
        'moe_b_up': nrm((DEPTH, N_EXPERTS, D_FF), 0.02),
        'moe_w_down': nrm((DEPTH, N_EXPERTS, D_FF, D_MODEL), D_FF ** -0.5),
        'moe_b_down': nrm((DEPTH, N_EXPERTS, D_MODEL), 0.02),
    }


def reference(x_prompt, x_sample, cache_na_k, cache_na_v, cache_mla_ckv, cache_mla_krope, cache_gqa_k,
              cache_gqa_v, state_s5_re, state_s5_im, c, c_ctx, norm_g, w_ada, b_ada, w_in, na_qk_g, na_rpb,
              mla_q_lora_g, mla_kv_lora_g, mla_w_q_up, mla_w_kv_up, mla_nope_g, mla_rope_g, gqa_qk_g,
              s5_lambda_re, s5_lambda_im, s5_log_step, s5_b_re, s5_b_im, s5_c_re, s5_c_im, s5_d, s5_w_glu,
              w_branch, w_out, moe_w_router, moe_b_router, moe_w_gate, moe_b_gate, moe_w_up, moe_b_up,
              moe_w_down, moe_b_down):
    def layer_params(l):
        return {
            'norm_g': norm_g[l], 'w_ada': w_ada[l], 'b_ada': b_ada[l], 'w_in': w_in[l],
            'na_qk_g': na_qk_g[l], 'na_rpb': na_rpb[l],
            'mla_q_lora_g': mla_q_lora_g[l], 'mla_kv_lora_g': mla_kv_lora_g[l],
            'mla_w_q_up': mla_w_q_up[l], 'mla_w_kv_up': mla_w_kv_up[l],
            'mla_nope_g': mla_nope_g[l], 'mla_rope_g': mla_rope_g[l], 'gqa_qk_g': gqa_qk_g[l],
            's5_lambda_re': s5_lambda_re[l], 's5_lambda_im': s5_lambda_im[l], 's5_log_step': s5_log_step[l],
            's5_b_re': s5_b_re[l], 's5_b_im': s5_b_im[l], 's5_c_re': s5_c_re[l], 's5_c_im': s5_c_im[l],
            's5_d': s5_d[l], 's5_w_glu': s5_w_glu[l], 'w_branch': w_branch[l], 'w_out': w_out[l],
            'moe_w_router': moe_w_router[l], 'moe_b_router': moe_b_router[l],
            'moe_w_gate': moe_w_gate[l], 'moe_b_gate': moe_b_gate[l], 'moe_w_up': moe_w_up[l],
            'moe_b_up': moe_b_up[l], 'moe_w_down': moe_w_down[l], 'moe_b_down': moe_b_down[l],
        }

    y_prompt = x_prompt
    ctx_out = []
    for l in range(DEPTH):
        y_prompt, ctx_l = trunk_layer(y_prompt, c_ctx[None, :], layer_params(l), None)
        ctx_out.append(ctx_l)
    new_na_k = jnp.stack([t[0] for t in ctx_out], axis=1)
    new_na_v = jnp.stack([t[1] for t in ctx_out], axis=1)
    new_mla_ckv = jnp.stack([t[2] for t in ctx_out], axis=1)
    new_mla_krope = jnp.stack([t[3] for t in ctx_out], axis=1)
    new_gqa_k = jnp.stack([t[4] for t in ctx_out], axis=1)
    new_gqa_v = jnp.stack([t[5] for t in ctx_out], axis=1)
    new_s5_re = jnp.stack([t[6] for t in ctx_out], axis=1)
    new_s5_im = jnp.stack([t[7] for t in ctx_out], axis=1)

    y_sample = x_sample
    for l in range(DEPTH):
        ctx_l = {'na_k': cache_na_k[:, l], 'na_v': cache_na_v[:, l],
                 'mla_ckv': cache_mla_ckv[:, l], 'mla_krope': cache_mla_krope[:, l],
                 'gqa_k': cache_gqa_k[:, l], 'gqa_v': cache_gqa_v[:, l],
                 's5_re': state_s5_re[:, l], 's5_im': state_s5_im[:, l]}
        y_sample, _ = trunk_layer(y_sample, c, layer_params(l), ctx_l)

    return (y_prompt, y_sample, new_na_k, new_na_v, new_mla_ckv, new_mla_krope, new_gqa_k, new_gqa_v,
            new_s5_re, new_s5_im)
```

```python
import functools
import math

import jax
import jax.numpy as jnp
import numpy as np
from jax import lax
from jax.experimental import pallas as pl
from jax.experimental.pallas import tpu as pltpu

F32 = jnp.float32
BF16 = jnp.bfloat16
U32 = jnp.uint32
I32 = jnp.int32

D_MODEL = 4096
BATCH = 32
SEQ = 256
DEPTH = 4
DEC_BATCH = 4
DEC_SEQ = 1024
PAST_LEN = 512
GRID_W = 64
HEAD_DIM = 128
N_BRANCHES = 4
BRANCH_WIDTH = D_MODEL // N_BRANCHES
ROPE_THETA = 10000.0
NEG_INF = -1e30
EPS = 1e-6
NA_HEADS = 8
NA_WIN_H = 8
NA_WIN_W = 16
MLA_HEADS = 8
MLA_NOPE = 128
MLA_ROPE = 64
MLA_Q_LORA = 1024
MLA_KV_LORA = 512
GQA_Q_HEADS = 8
GQA_KV_HEADS = 2
SSM_GROUP_CH = 16
SSM_GROUPS = 64
SSM_STATE = 64
N_EXPERTS = 32
TOP_K = 4
D_FF = 1024
SWIGLU_LIMIT = 7.0
SWIGLU_ALPHA = 1.702

N_CTX = BATCH * SEQ
N_LAT = DEC_BATCH * DEC_SEQ
N_TOK = N_CTX + N_LAT
MOD_ROWS = 8

COL_BKR = 3072 + MLA_Q_LORA + MLA_KV_LORA
COL_C = COL_BKR + MLA_ROPE
COL_GATE = COL_C + 1024 + 256 + 256 + 1024
IN_WIDTH = COL_GATE + N_BRANCHES * D_MODEL

LANE = 128
VMEM_LIMIT = 52 * 1024 * 1024

S5_GC = 4
S5_CHUNK_COLS = 2048

MOE_TM = 256
MOE_TILES = N_TOK * TOP_K // MOE_TM + N_EXPERTS
MOE_SLOTS = MOE_TILES * MOE_TM


def _cparams(n_axes, vmem=VMEM_LIMIT):
    return pltpu.CompilerParams(dimension_semantics=("arbitrary",) * n_axes, vmem_limit_bytes=vmem)


def _mod_row(row_start):
    return jnp.where(row_start < N_CTX, 0, 1 + (row_start - N_CTX) // DEC_SEQ)


def _cast_weight_tile(dst_ref, w_ref, w_next_ref, lane_shift, rows_per_step=256):
    k = w_ref.shape[0]
    rows = min(rows_per_step, k)

    def body(i, _):
        r = pl.multiple_of(i * rows, rows)
        w = w_ref[pl.ds(r, rows), :]
        if lane_shift:
            w = jnp.concatenate([w[:, lane_shift:], w_next_ref[pl.ds(r, rows), :lane_shift]], axis=1)
        dst_ref[pl.ds(r, rows), :] = w.astype(BF16)
        return 0

    lax.fori_loop(0, k // rows, body, 0)


def _mm_kernel(*refs, lane_shift, n_extra, epilogue):
    a_ref, w_ref = refs[0], refs[1]
    pos = 2
    w_next_ref = None
    if lane_shift:
        w_next_ref = refs[pos]
        pos += 1
    extra = refs[pos:pos + n_extra]
    o_ref = refs[pos + n_extra]
    wb_ref = refs[pos + n_extra + 1]

    @pl.when(pl.program_id(1) == 0)
    def _():
        _cast_weight_tile(wb_ref, w_ref, w_next_ref, lane_shift)

    acc = jnp.dot(a_ref[...].astype(BF16), wb_ref[...], preferred_element_type=F32)
    o_ref[...] = epilogue(acc, *[e[...] for e in extra]).astype(o_ref.dtype)


def _matmul(a, w, *, w_lead=(), col0=0, ncols=None, tm, tn, out_dtype, a_index_map=None, m_rows=None, k=None,
            extra=(), epilogue=None, name="matmul"):
    m_rows = m_rows or a.shape[0]
    a_index_map = a_index_map or (lambda n, m: (m, 0))
    k = k or w.shape[-2]
    ncols = ncols or w.shape[-1]
    lane_shift = col0 % LANE
    assert lane_shift in (0, 64) and m_rows % tm == 0 and ncols % tn == 0 and (col0 - lane_shift) % tn == 0
    nb0 = (col0 - lane_shift) // tn
    lead = tuple(w_lead)
    nlead = (None,) * len(lead)
    in_specs = [
        pl.BlockSpec((tm, k), a_index_map),
        pl.BlockSpec(nlead + (k, tn), lambda n, m: lead + (0, nb0 + n)),
    ]
    args = [a, w]
    if lane_shift:
        in_specs.append(pl.BlockSpec(nlead + (k, LANE), lambda n, m: lead + (0, (nb0 + n + 1) * (tn // LANE))))
        args.append(w)
    for arr, bshape, imap in extra:
        in_specs.append(pl.BlockSpec(bshape, imap))
        args.append(arr)
    epi = epilogue or (lambda acc: acc)
    return pl.pallas_call(
        functools.partial(_mm_kernel, lane_shift=lane_shift, n_extra=len(extra), epilogue=epi),
        grid=(ncols // tn, m_rows // tm),
        in_specs=in_specs,
        out_specs=pl.BlockSpec((tm, tn), lambda n, m: (m, n)),
        out_shape=jax.ShapeDtypeStruct((m_rows, ncols), out_dtype),
        scratch_shapes=[pltpu.VMEM((k, tn), BF16)],
        compiler_params=_cparams(2),
        name=name,
    )(*args)


def _ada_kernel(c_ref, w_ref, b_ref, o_ref, wb_ref):
    _cast_weight_tile(wb_ref, w_ref, None, 0)
    c = c_ref[...]
    s = (c * jax.nn.sigmoid(c)).astype(BF16)
    o_ref[...] = jnp.dot(s, wb_ref[...], preferred_element_type=F32) + b_ref[...]


def _ada_modulation(cond, w_ada, b_ada, tn=512):
    depth, d, n = w_ada.shape
    rows = cond.shape[0]
    return pl.pallas_call(
        _ada_kernel,
        grid=(depth, n // tn),
        in_specs=[
            pl.BlockSpec((rows, d), lambda l, j: (0, 0)),
            pl.BlockSpec((None, d, tn), lambda l, j: (l, 0, j)),
            pl.BlockSpec((None, 1, tn), lambda l, j: (l, 0, j)),
        ],
        out_specs=pl.BlockSpec((None, rows, tn), lambda l, j: (l, 0, j)),
        out_shape=jax.ShapeDtypeStruct((depth, rows, n), F32),
        scratch_shapes=[pltpu.VMEM((d, tn), BF16)],
        compiler_params=_cparams(2),
        name="ada_modulation",
    )(cond, w_ada, b_ada.reshape(depth, 1, n))


def _pack_bf16_pair(lo, hi):
    lo_bits = lax.bitcast_convert_type(lo.astype(BF16).astype(F32), U32) >> 16
    hi_bits = lax.bitcast_convert_type(hi.astype(BF16).astype(F32), U32) & jnp.uint32(0xFFFF0000)
    return lo_bits | hi_bits


def _unpack_bf16_pair(word):
    lo = lax.bitcast_convert_type(word << 16, F32)
    hi = lax.bitcast_convert_type(word & jnp.uint32(0xFFFF0000), F32)
    return lo, hi


def _modulated_norm(x, g, scale, shift):
    y = x * lax.rsqrt(jnp.mean(x * x, axis=-1, keepdims=True) + EPS)
    return (y * g) * (1.0 + scale) + shift


def _norm_kernel(x_ref, g_ref, shift_ref, scale_ref, o_ref):
    o_ref[...] = _modulated_norm(x_ref[...], g_ref[...], scale_ref[...], shift_ref[...]).astype(o_ref.dtype)


def _split_hi_lo(v):
    hi = v.astype(BF16)
    return hi, (v - hi.astype(F32)).astype(BF16)


def _norm_router_kernel(x_ref, g_ref, shift_ref, scale_ref, wr_hi_ref, wr_lo_ref, br_ref,
                        hp_ref, e_ref, p_ref):
    h = _modulated_norm(x_ref[...], g_ref[...], scale_ref[...], shift_ref[...])
    half = h.shape[1] // 2
    hp_ref[...] = _pack_bf16_pair(h[:, :half], h[:, half:])
    h_hi, h_lo = _split_hi_lo(h)
    logits = (jnp.dot(h_hi, wr_hi_ref[...], preferred_element_type=F32)
              + jnp.dot(h_hi, wr_lo_ref[...], preferred_element_type=F32)
              + jnp.dot(h_lo, wr_hi_ref[...], preferred_element_type=F32)) + br_ref[...]
    lane = lax.broadcasted_iota(I32, logits.shape, 1)
    logits = jnp.where(lane < N_EXPERTS, logits, -jnp.inf)
    e_out = jnp.zeros(logits.shape, I32)
    p_out = jnp.zeros(logits.shape, F32)
    top = None
    denom = None
    for kk in range(TOP_K):
        m = jnp.max(logits, axis=-1, keepdims=True)
        idx = jnp.min(jnp.where(logits == m, lane, LANE), axis=-1, keepdims=True)
        if kk == 0:
            top = m
        p = jnp.exp(m - top)
        denom = p if kk == 0 else denom + p
        e_out = jnp.where(lane == kk, idx, e_out)
        p_out = jnp.where(lane == kk, p, p_out)
        logits = jnp.where(lane == idx, -jnp.inf, logits)
    e_ref[...] = e_out
    p_ref[...] = p_out / denom


def _norm_specs(tm, d, layer, which):
    return [
        pl.BlockSpec((tm, d), lambda i: (i, 0)),
        pl.BlockSpec((None, None, 1, d), lambda i: (layer, which, 0, 0)),
        pl.BlockSpec((None, None, 1, d), lambda i: (layer, _mod_row(i * tm), 0, 3 * which)),
        pl.BlockSpec((None, None, 1, d), lambda i: (layer, _mod_row(i * tm), 0, 3 * which + 1)),
    ]


def _norm1(x, norm_g4, mod4, layer, tm=256):
    n, d = x.shape
    return pl.pallas_call(
        _norm_kernel,
        grid=(n // tm,),
        in_specs=_norm_specs(tm, d, layer, 0),
        out_specs=pl.BlockSpec((tm, d), lambda i: (i, 0)),
        out_shape=jax.ShapeDtypeStruct((n, d), BF16),
        compiler_params=_cparams(1),
        name="norm1",
    )(x, norm_g4, mod4, mod4)


def _norm2_router(x, norm_g4, mod4, wr_hi, wr_lo, br, layer, tm=256):
    n, d = x.shape
    const = lambda i: (layer, 0, 0)
    return pl.pallas_call(
        _norm_router_kernel,
        grid=(n // tm,),
        in_specs=_norm_specs(tm, d, layer, 1) + [
            pl.BlockSpec((None, d, LANE), const),
            pl.BlockSpec((None, d, LANE), const),
            pl.BlockSpec((None, 1, LANE), const),
        ],
        out_specs=[
            pl.BlockSpec((tm, d // 2), lambda i: (i, 0)),
            pl.BlockSpec((tm, LANE), lambda i: (i, 0)),
            pl.BlockSpec((tm, LANE), lambda i: (i, 0)),
        ],
        out_shape=[
            jax.ShapeDtypeStruct((n, d // 2), U32),
            jax.ShapeDtypeStruct((n, LANE), I32),
            jax.ShapeDtypeStruct((n, LANE), F32),
        ],
        compiler_params=_cparams(1),
        name="norm2_router",
    )(x, norm_g4, mod4, mod4, wr_hi, wr_lo, br)


def _headnorm_kernel(*refs, heads, dh, dh_eff, rope_half, f32_width):
    x_ref, g_ref = refs[0], refs[1]
    pos = 2
    tabs = None
    if rope_half:
        tabs = [r[...] for r in refs[pos:pos + 3]]
        pos += 3
    ob_ref = refs[pos]
    of_ref = refs[pos + 1] if f32_width else None
    g = g_ref[...]
    for h in range(heads):
        x = x_ref[:, h * dh:(h + 1) * dh].astype(F32)
        if dh_eff < dh:
            lane = lax.broadcasted_iota(I32, x.shape, 1)
            x = jnp.where(lane < dh_eff, x, 0.0)
        y = x * lax.rsqrt(jnp.sum(x * x, axis=-1, keepdims=True) * (1.0 / dh_eff) + EPS) * g
        if rope_half:
            c, sa, sb = tabs
            y = y * c + pltpu.roll(y, rope_half, 1) * sa + pltpu.roll(y, dh - rope_half, 1) * sb
        ob_ref[:, h * dh:(h + 1) * dh] = y.astype(ob_ref.dtype)
        if of_ref is not None:
            of_ref[:, h * f32_width:(h + 1) * f32_width] = y[:, :f32_width]


def _headnorm(x, gain, *, col_block, heads, dh=HEAD_DIM, dh_eff=None, rope=None, rope_half=0,
              f32_width=0, tm=256, name="headnorm"):
    n = x.shape[0]
    w = heads * dh
    dh_eff = dh_eff or dh
    in_specs = [pl.BlockSpec((tm, w), lambda i: (i, col_block)), pl.BlockSpec((1, dh), lambda i: (0, 0))]
    args = [x, gain]
    if rope is not None:
        in_specs += [pl.BlockSpec((tm, dh), lambda i: (i, 0))] * 3
        args += list(rope)
    out_specs = [pl.BlockSpec((tm, w), lambda i: (i, 0))]
    out_shape = [jax.ShapeDtypeStruct((n, w), BF16)]
    if f32_width:
        out_specs.append(pl.BlockSpec((tm, heads * f32_width), lambda i: (i, 0)))
        out_shape.append(jax.ShapeDtypeStruct((n, heads * f32_width), F32))
    out = pl.pallas_call(
        functools.partial(_headnorm_kernel, heads=heads, dh=dh, dh_eff=dh_eff,
                          rope_half=rope_half if rope is not None else 0, f32_width=f32_width),
        grid=(n // tm,),
        in_specs=in_specs,
        out_specs=out_specs,
        out_shape=out_shape,
        compiler_params=_cparams(1),
        name=name,
    )(*args)
    return out if f32_width else out[0]


def _attn_kernel(*refs, heads, n_parts, k_heads, v_heads, n_src, has_bias):
    d = HEAD_DIM
    q_refs = refs[:n_parts]
    pos = n_parts
    srcs = []
    for _ in range(n_src):
        srcs.append((refs[pos:pos + n_parts], refs[pos + n_parts]))
        pos += n_parts + 1
    bias_ref = refs[pos] if has_bias else None
    o_ref = refs[-1]
    contract_last = (((1,), (1,)), ((), ()))
    for h in range(heads):
        scores = []
        for si, (k_refs, _) in enumerate(srcs):
            s = None
            for p in range(n_parts):
                kh = h // (heads // k_heads[p])
                q = q_refs[p][:, h * d:(h + 1) * d]
                k = k_refs[p][:, kh * d:(kh + 1) * d].astype(BF16)
                part = lax.dot_general(q, k, contract_last, preferred_element_type=F32)
                s = part if s is None else s + part
            if si == 0 and has_bias:
                s = s + bias_ref[h]
            scores.append(s)
        m = functools.reduce(jnp.maximum, [jnp.max(s, axis=-1, keepdims=True) for s in scores])
        probs = [jnp.exp(s - m) for s in scores]
        denom = functools.reduce(lambda a, b: a + b, [jnp.sum(p, axis=-1, keepdims=True) for p in probs])
        vh = h // (heads // v_heads)
        o = None
        for p, (_, v_ref) in zip(probs, srcs):
            v = v_ref[:, vh * d:(vh + 1) * d].astype(BF16)
            pv = jnp.dot(p.astype(BF16), v, preferred_element_type=F32)
            o = pv if o is None else o + pv
        o_ref[:, h * d:(h + 1) * d] = (o / denom).astype(o_ref.dtype)


def _attention(q_parts, sources, *, n_batch, seq, tq, q_row0, heads, k_heads, v_heads, bias=None, name="attention"):
    w = heads * HEAD_DIM
    nq = seq // tq
    qb0 = q_row0 // tq
    in_specs, args = [], []
    for arr, cb in q_parts:
        in_specs.append(pl.BlockSpec((tq, w), lambda b, qi, cb=cb: (qb0 + b * nq + qi, cb)))
        args.append(arr)
    for k_parts, v in sources:
        for arr, bshape, imap in list(k_parts) + [v]:
            in_specs.append(pl.BlockSpec(bshape, imap))
            args.append(arr)
    if bias is not None:
        in_specs.append(pl.BlockSpec(bias[1], bias[2]))
        args.append(bias[0])
    return pl.pallas_call(
        functools.partial(_attn_kernel, heads=heads, n_parts=len(q_parts), k_heads=k_heads, v_heads=v_heads,
                          n_src=len(sources), has_bias=bias is not None),
        grid=(n_batch, nq),
        in_specs=in_specs,
        out_specs=pl.BlockSpec((tq, w), lambda b, qi: (b * nq + qi, 0)),
        out_shape=jax.ShapeDtypeStruct((n_batch * seq, w), BF16),
        compiler_params=_cparams(2),
        name=name,
    )(*args)


S5_HALF_ROWS = 64


def _s5_step(h, bu, lam):
    h4 = h.reshape(S5_GC, 2, 8, LANE)
    b4 = bu.reshape(S5_GC, 2, 8, LANE)
    a4 = lam.reshape(S5_GC, 2, 8, LANE)
    hr, hi = h4[:, 0], h4[:, 1]
    ar, ai = a4[:, 0], a4[:, 1]
    nr = ar * hr - ai * hi + b4[:, 0]
    ni = ar * hi + ai * hr + b4[:, 1]
    return jnp.stack([nr, ni], axis=1).reshape(S5_HALF_ROWS, LANE)


def _s5_scan_kernel(buf_ref, bub_ref, lam_ref, h0_ref, hf_ref, hb_ref, fin_ref, state_ref, *, tc):
    j = pl.program_id(1)

    @pl.when(j == 0)
    def _():
        state_ref[...] = h0_ref[...]

    lam_f = lam_ref[:S5_HALF_ROWS, :]
    lam_b = lam_ref[S5_HALF_ROWS:, :]

    def body(i, carry):
        hf, hb = carry
        hf = _s5_step(hf, buf_ref[i], lam_f)
        hb = _s5_step(hb, bub_ref[tc - 1 - i], lam_b)
        hf_ref[i] = hf.astype(hf_ref.dtype)
        hb_ref[tc - 1 - i] = hb.astype(hb_ref.dtype)
        return hf, hb

    hf, hb = lax.fori_loop(0, tc, body, (state_ref[:S5_HALF_ROWS, :], state_ref[S5_HALF_ROWS:, :]), unroll=4)
    state_ref[:S5_HALF_ROWS, :] = hf
    state_ref[S5_HALF_ROWS:, :] = hb

    @pl.when(j == pl.num_programs(1) - 1)
    def _():
        fin_ref[...] = state_ref[...]


def _s5_scan(bu3, lam_tiles, h0, *, layer, n_batch, seq, row0, tc=64):
    nt = seq // tc
    rb0 = row0 // tc
    n = n_batch * seq
    return pl.pallas_call(
        functools.partial(_s5_scan_kernel, tc=tc),
        grid=(n_batch, nt),
        in_specs=[
            pl.BlockSpec((tc, S5_HALF_ROWS, LANE), lambda b, j: (rb0 + b * nt + j, 0, 0)),
            pl.BlockSpec((tc, S5_HALF_ROWS, LANE), lambda b, j: (rb0 + b * nt + (nt - 1 - j), 1, 0)),
            pl.BlockSpec((None, 2 * S5_HALF_ROWS, LANE), lambda b, j: (layer, 0, 0)),
            pl.BlockSpec((None, 2 * S5_HALF_ROWS, LANE), lambda b, j: (b, 0, 0)),
        ],
        out_specs=[
            pl.BlockSpec((tc, S5_HALF_ROWS, LANE), lambda b, j: (b * nt + j, 0, 0)),
            pl.BlockSpec((tc, S5_HALF_ROWS, LANE), lambda b, j: (b * nt + (nt - 1 - j), 0, 0)),
            pl.BlockSpec((None, 2 * S5_HALF_ROWS, LANE), lambda b, j: (b, 0, 0)),
        ],
        out_shape=[
            jax.ShapeDtypeStruct((n, S5_HALF_ROWS, LANE), BF16),
            jax.ShapeDtypeStruct((n, S5_HALF_ROWS, LANE), BF16),
            jax.ShapeDtypeStruct((n_batch, 2 * S5_HALF_ROWS, LANE), F32),
        ],
        scratch_shapes=[pltpu.VMEM((2 * S5_HALF_ROWS, LANE), F32)],
        compiler_params=_cparams(2),
        name="s5_scan",
    )(bu3, bu3, lam_tiles, h0)


def _multi_dot_kernel(*refs, n_terms, gated, n_extra, epilogue):
    a_refs = refs[:n_terms]
    w_ref = refs[n_terms]
    pos = n_terms + 1
    g_refs = refs[pos:pos + n_terms] if gated else None
    pos += n_terms if gated else 0
    extra = refs[pos:pos + n_extra]
    o_ref = refs[pos + n_extra]
    wb_ref = refs[pos + n_extra + 1]

    @pl.when(pl.program_id(1) == 0)
    def _():
        for t in range(n_terms):
            _cast_weight_tile(wb_ref.at[t], w_ref.at[t], None, 0)

    acc = None
    for t in range(n_terms):
        part = jnp.dot(a_refs[t][...].astype(BF16), wb_ref[t], preferred_element_type=F32)
        if gated:
            part = part * jax.nn.sigmoid(g_refs[t][...].astype(F32))
        acc = part if acc is None else acc + part
    o_ref[...] = epilogue(acc, *[e[...] for e in extra]).astype(o_ref.dtype)


def _multi_dot(a_terms, w, w_spec, *, m_rows, ncols, tm, tn, out_dtype, gates=None, extra=(), epilogue=None,
               name="multi_dot"):
    n_terms = len(a_terms)
    kdim = a_terms[0][1][1]
    in_specs = [pl.BlockSpec(bs, im) for _, bs, im in a_terms] + [w_spec]
    args = [a for a, _, _ in a_terms] + [w]
    if gates is not None:
        in_specs += [pl.BlockSpec(bs, im) for _, bs, im in gates]
        args += [g for g, _, _ in gates]
    for arr, bshape, imap in extra:
        in_specs.append(pl.BlockSpec(bshape, imap))
        args.append(arr)
    epi = epilogue or (lambda acc: acc)
    return pl.pallas_call(
        functools.partial(_multi_dot_kernel, n_terms=n_terms, gated=gates is not None, n_extra=len(extra), epilogue=epi),
        grid=(ncols // tn, m_rows // tm),
        in_specs=in_specs,
        out_specs=pl.BlockSpec((tm, tn), lambda n, m: (m, n)),
        out_shape=jax.ShapeDtypeStruct((m_rows, ncols), out_dtype),
        scratch_shapes=[pltpu.VMEM((n_terms, kdim, tn), BF16)],
        compiler_params=_cparams(2),
        name=name,
    )(*args)


def _row_gather(idx_of, src_hbm, dst_of, sem, n_rows):
    def copy(r):
        return pltpu.make_async_copy(src_hbm.at[pl.ds(idx_of(r), 1)], dst_of(r), sem)

    def start():
        lax.fori_loop(0, n_rows, lambda r, c: (copy(r).start(), c)[1], 0)

    def wait():
        lax.fori_loop(0, n_rows, lambda r, c: (copy(r).wait(), c)[1], 0)

    return start, wait


def _moe_gather_kernel(tok_ref, nv_ref, hp_hbm, o_ref, buf, sem, *, tm):
    t = pl.program_id(0)
    nv = nv_ref[0]

    def tile_ops(tile, slot):
        return _row_gather(lambda r: tok_ref[tile * tm + r], hp_hbm,
                           lambda r: buf.at[slot, pl.ds(r, 1)], sem.at[slot], tm)

    @pl.when((t == 0) & (nv > 0))
    def _():
        tile_ops(0, 0)[0]()

    @pl.when(t + 1 < nv)
    def _():
        tile_ops(t + 1, (t + 1) % 2)[0]()

    @pl.when(t < nv)
    def _():
        slot = t % 2
        tile_ops(t, slot)[1]()
        lo, hi = _unpack_bf16_pair(buf[slot])
        half = lo.shape[1]
        o_ref[:, :half] = lo.astype(BF16)
        o_ref[:, half:] = hi.astype(BF16)

    @pl.when(t >= nv)
    def _():
        o_ref[...] = jnp.zeros_like(o_ref)


def _moe_gather(slot_tok, n_valid, hp, tm=MOE_TM):
    n_slots = slot_tok.shape[0]
    half = hp.shape[1]
    return pl.pallas_call(
        functools.partial(_moe_gather_kernel, tm=tm),
        grid_spec=pltpu.PrefetchScalarGridSpec(
            num_scalar_prefetch=2,
            grid=(n_slots // tm,),
            in_specs=[pl.BlockSpec(memory_space=pl.ANY)],
            out_specs=pl.BlockSpec((tm, 2 * half), lambda t, tok, nv: (t, 0)),
            scratch_shapes=[pltpu.VMEM((2, tm, half), U32), pltpu.SemaphoreType.DMA((2,))],
        ),
        out_shape=jax.ShapeDtypeStruct((n_slots, 2 * half), BF16),
        compiler_params=_cparams(1),
        name="moe_gather",
    )(slot_tok, n_valid, hp)


def _moe_tile_flags(te_ref, nv_ref, t):
    valid = t < nv_ref[0]
    new_expert = (t == 0) | (te_ref[t] != te_ref[jnp.maximum(t - 1, 0)])
    return valid, valid & new_expert


def _moe_up_kernel(te_ref, nv_ref, xs_ref, wg_ref, wu_ref, bg_ref, bu_ref, o_ref, wb_ref):
    valid, recast = _moe_tile_flags(te_ref, nv_ref, pl.program_id(1))

    @pl.when(recast)
    def _():
        _cast_weight_tile(wb_ref.at[0], wg_ref, None, 0)
        _cast_weight_tile(wb_ref.at[1], wu_ref, None, 0)

    @pl.when(valid)
    def _():
        x = xs_ref[...]
        gate = jnp.minimum(jnp.dot(x, wb_ref[0], preferred_element_type=F32) + bg_ref[...], SWIGLU_LIMIT)
        up = jnp.clip(jnp.dot(x, wb_ref[1], preferred_element_type=F32) + bu_ref[...], -SWIGLU_LIMIT, SWIGLU_LIMIT)
        o_ref[...] = (gate * jax.nn.sigmoid(SWIGLU_ALPHA * gate) * (up + 1.0)).astype(o_ref.dtype)

    @pl.when(jnp.logical_not(valid))
    def _():
        o_ref[...] = jnp.zeros_like(o_ref)


def _moe_down_kernel(te_ref, nv_ref, act_ref, wd_ref, bd_ref, o_ref, wb_ref):
    valid, recast = _moe_tile_flags(te_ref, nv_ref, pl.program_id(1))

    @pl.when(recast)
    def _():
        _cast_weight_tile(wb_ref, wd_ref, None, 0)

    @pl.when(valid)
    def _():
        y = jnp.dot(act_ref[...], wb_ref[...], preferred_element_type=F32) + bd_ref[...]
        half = y.shape[1] // 2
        o_ref[...] = _pack_bf16_pair(y[:, :half], y[:, half:])

    @pl.when(jnp.logical_not(valid))
    def _():
        o_ref[...] = jnp.zeros_like(o_ref)


def _moe_experts(tile_e, n_valid, xs, w_gate, b_gate, w_up, b_up, w_down, b_down, layer, tm=MOE_TM, tf=256, tn=2048):
    n_slots, d = xs.shape
    d_ff = w_gate.shape[-1]
    n_tiles = n_slots // tm
    row = lambda t, nv: jnp.minimum(t, jnp.maximum(nv[0] - 1, 0))
    act = pl.pallas_call(
        _moe_up_kernel,
        grid_spec=pltpu.PrefetchScalarGridSpec(
            num_scalar_prefetch=2,
            grid=(d_ff // tf, n_tiles),
            in_specs=[
                pl.BlockSpec((tm, d), lambda f, t, te, nv: (row(t, nv), 0)),
                pl.BlockSpec((None, None, d, tf), lambda f, t, te, nv: (layer, te[t], 0, f)),
                pl.BlockSpec((None, None, d, tf), lambda f, t, te, nv: (layer, te[t], 0, f)),
                pl.BlockSpec((None, None, 1, tf), lambda f, t, te, nv: (layer, te[t], 0, f)),
                pl.BlockSpec((None, None, 1, tf), lambda f, t, te, nv: (layer, te[t], 0, f)),
            ],
            out_specs=pl.BlockSpec((tm, tf), lambda f, t, te, nv: (t, f)),
            scratch_shapes=[pltpu.VMEM((2, d, tf), BF16)],
        ),
        out_shape=jax.ShapeDtypeStruct((n_slots, d_ff), BF16),
        compiler_params=_cparams(2),
        name="moe_up",
    )(tile_e, n_valid, xs, w_gate, w_up, b_gate, b_up)
    return pl.pallas_call(
        _moe_down_kernel,
        grid_spec=pltpu.PrefetchScalarGridSpec(
            num_scalar_prefetch=2,
            grid=(d // tn, n_tiles),
            in_specs=[
                pl.BlockSpec((tm, d_ff), lambda n, t, te, nv: (row(t, nv), 0)),
                pl.BlockSpec((None, None, d_ff, tn), lambda n, t, te, nv: (layer, te[t], 0, n)),
                pl.BlockSpec((None, None, 1, tn), lambda n, t, te, nv: (layer, te[t], 0, n)),
            ],
            out_specs=pl.BlockSpec((tm, tn // 2), lambda n, t, te, nv: (t, n)),
            scratch_shapes=[pltpu.VMEM((d_ff, tn), BF16)],
        ),
        out_shape=jax.ShapeDtypeStruct((n_slots, d // 2), U32),
        compiler_params=_cparams(2),
        name="moe_down",
    )(tile_e, n_valid, act, w_down, b_down)


def _moe_combine_kernel(slot_ref, ys_hbm, x_ref, p_ref, gate_ref, o_ref, buf, sem, *, tt, tn):
    i = pl.program_id(0)
    n_steps = pl.num_programs(0)
    rows = tt * TOP_K

    def tile_ops(tile, slot):
        return _row_gather(lambda r: slot_ref[tile * rows + r], ys_hbm,
                           lambda r: buf.at[slot, r % TOP_K, pl.ds(r // TOP_K, 1)], sem.at[slot], rows)

    @pl.when(i == 0)
    def _():
        tile_ops(0, 0)[0]()

    @pl.when(i + 1 < n_steps)
    def _():
        tile_ops(i + 1, (i + 1) % 2)[0]()

    slot = i % 2
    tile_ops(i, slot)[1]()
    p = p_ref[...]
    hw = tn // 2
    for n in range(x_ref.shape[1] // tn):
        lo = hi = None
        for kk in range(TOP_K):
            w_lo, w_hi = _unpack_bf16_pair(buf[slot, kk, :, n * hw:(n + 1) * hw])
            pk = p[:, kk:kk + 1]
            lo = w_lo * pk if lo is None else lo + w_lo * pk
            hi = w_hi * pk if hi is None else hi + w_hi * pk
        c0 = n * tn
        o_ref[:, c0:c0 + hw] = x_ref[:, c0:c0 + hw] + gate_ref[:, c0:c0 + hw] * lo
        o_ref[:, c0 + hw:c0 + tn] = x_ref[:, c0 + hw:c0 + tn] + gate_ref[:, c0 + hw:c0 + tn] * hi


def _moe_combine(slots, ys, x, p_sel, mod4, layer, tt=128, tn=2048):
    n, d = x.shape
    return pl.pallas_call(
        functools.partial(_moe_combine_kernel, tt=tt, tn=tn),
        grid_spec=pltpu.PrefetchScalarGridSpec(
            num_scalar_prefetch=1,
            grid=(n // tt,),
            in_specs=[
                pl.BlockSpec(memory_space=pl.ANY),
                pl.BlockSpec((tt, d), lambda i, s: (i, 0)),
                pl.BlockSpec((tt, LANE), lambda i, s: (i, 0)),
                pl.BlockSpec((None, None, 1, d), lambda i, s: (layer, _mod_row(i * tt), 0, 5)),
            ],
            out_specs=pl.BlockSpec((tt, d), lambda i, s: (i, 0)),
            scratch_shapes=[pltpu.VMEM((2, TOP_K, tt, d // 2), U32), pltpu.SemaphoreType.DMA((2,))],
        ),
        out_shape=jax.ShapeDtypeStruct((n, d), F32),
        compiler_params=_cparams(1),
        name="moe_combine",
    )(slots, ys, x, p_sel, mod4)


def _rope_tables(rot_dim):
    t = jnp.arange(DEC_SEQ)
    n_freq = rot_dim // 4
    freqs = ROPE_THETA ** (-jnp.arange(n_freq, dtype=F32) / n_freq)
    ang = jnp.concatenate([(t // GRID_W).astype(F32)[:, None] * freqs,
                           (t % GRID_W).astype(F32)[:, None] * freqs], axis=-1)
    cos, sin = jnp.cos(ang), jnp.sin(ang)
    half = rot_dim // 2
    pad = HEAD_DIM - rot_dim
    zeros = jnp.zeros_like(sin)

    def full(lat, ctx_row):
        lat = jnp.pad(lat, ((0, 0), (0, pad)))
        ctx = jnp.broadcast_to(jnp.pad(ctx_row, (0, pad))[None], (N_CTX, HEAD_DIM))
        return jnp.concatenate([ctx, jnp.tile(lat, (DEC_BATCH, 1))], axis=0)

    one = jnp.ones((rot_dim,), F32)
    zero = jnp.zeros((rot_dim,), F32)
    c = full(jnp.concatenate([cos, cos], axis=-1), one)
    sa = full(jnp.concatenate([zeros, sin], axis=-1), zero)
    sb = full(jnp.concatenate([-sin, zeros], axis=-1), zero)
    return (c, sa, sb), half


def _na_bias(na_rpb):
    rows = DEC_SEQ // GRID_W
    wh = min(NA_WIN_H, rows)
    r = np.arange(rows)
    row_lo = np.clip(r - wh // 2, 0, rows - wh)
    row_ok = (r[None, :] >= row_lo[:, None]) & (r[None, :] < row_lo[:, None] + wh)
    d_row = np.clip(r[None, :] - r[:, None] + (NA_WIN_H - 1), 0, 2 * NA_WIN_H - 2)
    col = np.arange(GRID_W)
    col_lo = np.clip(col - NA_WIN_W // 2, 0, GRID_W - NA_WIN_W)
    col_ok = (col[None, :] >= col_lo[:, None]) & (col[None, :] < col_lo[:, None] + NA_WIN_W)
    d_col = np.clip(col[None, :] - col[:, None], -(NA_WIN_W - 1), NA_WIN_W - 1) + (NA_WIN_W - 1)
    sel_row = (d_row[:, :, None] == np.arange(2 * NA_WIN_H - 1)).astype(np.float32)
    sel_col = (d_col[:, :, None] == np.arange(2 * NA_WIN_W - 1)).astype(np.float32)
    rpb = na_rpb.astype(F32)
    t1 = jnp.einsum('lhad,rsa->lhrsd', rpb, sel_row, precision=lax.Precision.HIGHEST)
    t2 = jnp.einsum('lhrsd,qkd->lhrqsk', t1, sel_col, precision=lax.Precision.HIGHEST)
    ok = row_ok[:, None, :, None] & col_ok[None, :, None, :]
    bias = jnp.where(ok[None, None], t2, NEG_INF)
    return bias.reshape(DEPTH, NA_HEADS, DEC_SEQ, DEC_SEQ)


def _s5_params(lam_re, lam_im, log_step, b_re, b_im, c_re, c_im):
    step = jnp.exp(log_step.astype(F32))[..., None]
    lr, li = lam_re.astype(F32), lam_im.astype(F32)
    mag = jnp.exp(lr * step)
    ar, ai = mag * jnp.cos(li * step), mag * jnp.sin(li * step)
    den = lr * lr + li * li
    kr = ((ar - 1.0) * lr + ai * li) / den
    ki = (ai * lr - (ar - 1.0) * li) / den
    bbr = kr[..., None] * b_re - ki[..., None] * b_im
    bbi = kr[..., None] * b_im + ki[..., None] * b_re
    gl = SSM_GROUPS // S5_GC
    eye = jnp.eye(gl, dtype=F32)

    def tiles(re, im):
        v = jnp.stack([re, im], axis=3)
        v = v.reshape(DEPTH, 2, S5_GC, gl, 2, SSM_STATE).transpose(0, 1, 2, 4, 3, 5)
        return v.reshape(DEPTH, 2 * S5_HALF_ROWS, LANE)

    lam_tiles = tiles(ar, ai)
    bb = jnp.stack([bbr, bbi], axis=-1)
    bb = bb.reshape(DEPTH, 2, S5_GC, gl, SSM_STATE, SSM_GROUP_CH, 2)
    wb = bb.transpose(0, 1, 2, 3, 5, 6, 4)[:, :, :, :, :, :, None, :] * eye[None, None, None, :, None, None, :, None]
    wb = wb.reshape(DEPTH, 2 * S5_GC, gl * SSM_GROUP_CH, S5_CHUNK_COLS)
    wb = wb.transpose(0, 2, 1, 3).reshape(DEPTH, gl * SSM_GROUP_CH, 2 * S5_GC * S5_CHUNK_COLS)
    cc = jnp.stack([c_re.astype(F32), -c_im.astype(F32)], axis=2)
    cc = cc.reshape(DEPTH, 2, 2, S5_GC, gl, SSM_GROUP_CH, SSM_STATE)
    wc = cc.transpose(0, 1, 3, 2, 4, 6, 5)[:, :, :, :, :, :, None, :] * eye[None, None, None, None, :, None, :, None]
    wc = wc.reshape(DEPTH, 2, S5_GC, S5_CHUNK_COLS, gl * SSM_GROUP_CH)
    return lam_tiles, wb.astype(BF16), wc.astype(BF16)


def _s5_state_tiles(re, im):
    gl = SSM_GROUPS // S5_GC
    v = jnp.stack([re.astype(F32), im.astype(F32)], axis=3)
    v = v.reshape(-1, 2, S5_GC, gl, 2, SSM_STATE).transpose(0, 1, 2, 4, 3, 5)
    return v.reshape(-1, 2 * S5_HALF_ROWS, LANE)


def _s5_split_tiles(tiles):
    gl = SSM_GROUPS // S5_GC
    v = tiles.reshape(-1, 2, S5_GC, 2, gl, SSM_STATE).transpose(0, 1, 3, 2, 4, 5)
    v = v.reshape(-1, 2, 2, SSM_GROUPS, SSM_STATE)
    return v[:, :, 0], v[:, :, 1]


def _moe_routing(e_sel, tm=MOE_TM):
    flat_e = e_sel[:, :TOP_K].reshape(-1)
    onehot = (flat_e[:, None] == jnp.arange(N_EXPERTS, dtype=I32)[None, :]).astype(I32)
    csum = jnp.cumsum(onehot, axis=0)
    rank = jnp.sum(onehot * (csum - 1), axis=1)
    counts = csum[-1]
    padded = (counts + tm - 1) // tm * tm
    pad_end = jnp.cumsum(padded)
    pad_start = pad_end - padded
    slot = (pad_start[flat_e] + rank).astype(I32)
    n_assign = flat_e.shape[0]
    n_tiles = n_assign // tm + N_EXPERTS
    slot_tok = jnp.zeros((n_tiles * tm,), I32).at[slot].set(jnp.arange(n_assign, dtype=I32) // TOP_K)
    tile_e = jnp.minimum(jnp.searchsorted(pad_end, jnp.arange(n_tiles, dtype=I32) * tm, side='right'),
                         N_EXPERTS - 1).astype(I32)
    n_valid = (pad_end[-1] // tm).astype(I32).reshape(1)
    return slot, slot_tok, tile_e, n_valid


def kernel(x_prompt, x_sample, cache_na_k, cache_na_v, cache_mla_ckv, cache_mla_krope, cache_gqa_k, cache_gqa_v,
           state_s5_re, state_s5_im, c, c_ctx, norm_g, w_ada, b_ada, w_in, na_qk_g, na_rpb, mla_q_lora_g,
           mla_kv_lora_g, mla_w_q_up, mla_w_kv_up, mla_nope_g, mla_rope_g, gqa_qk_g, s5_lambda_re, s5_lambda_im,
           s5_log_step, s5_b_re, s5_b_im, s5_c_re, s5_c_im, s5_d, s5_w_glu, w_branch, w_out, moe_w_router,
           moe_b_router, moe_w_gate, moe_b_gate, moe_w_up, moe_b_up, moe_w_down, moe_b_down):
    L, D = DEPTH, D_MODEL
    tm = 512
    ctx_blocks_1k = N_CTX // DEC_SEQ

    x = jnp.concatenate([x_prompt.reshape(N_CTX, D), x_sample.reshape(N_LAT, D)], axis=0).astype(F32)
    cond = jnp.zeros((MOD_ROWS, D), F32).at[0].set(c_ctx.astype(F32)).at[1:1 + DEC_BATCH].set(c.astype(F32))
    mod4 = _ada_modulation(cond, w_ada, b_ada).reshape(L, MOD_ROWS, 1, 6 * D)
    norm_g4 = norm_g.astype(F32).reshape(L, 2, 1, D)

    mla_tabs, mla_half = _rope_tables(MLA_ROPE)
    gqa_tabs, gqa_half = _rope_tables(HEAD_DIM)
    na_bias = _na_bias(na_rpb)
    lam_tiles, s5_wb, s5_wc = _s5_params(s5_lambda_re, s5_lambda_im, s5_log_step, s5_b_re, s5_b_im, s5_c_re, s5_c_im)
    h0_lat = _s5_state_tiles(state_s5_re.reshape(-1, 2, SSM_GROUPS, SSM_STATE),
                             state_s5_im.reshape(-1, 2, SSM_GROUPS, SSM_STATE)).reshape(DEC_BATCH, L, 128, LANE)
    h0_ctx = jnp.zeros((BATCH, 128, LANE), F32)
    s5_d3 = s5_d.astype(F32).reshape(L, 1, -1)

    wq = mla_w_q_up.reshape(L, MLA_Q_LORA, MLA_HEADS, MLA_NOPE + MLA_ROPE)
    wq_perm = jnp.concatenate([
        wq[..., :MLA_NOPE].reshape(L, MLA_Q_LORA, MLA_HEADS * MLA_NOPE),
        jnp.pad(wq[..., MLA_NOPE:], ((0, 0), (0, 0), (0, 0), (0, HEAD_DIM - MLA_ROPE))).reshape(L, MLA_Q_LORA, -1),
    ], axis=-1)
    wkv_perm = mla_w_kv_up.reshape(L, MLA_KV_LORA, MLA_HEADS, 2, HEAD_DIM).transpose(0, 1, 3, 2, 4).reshape(
        L, MLA_KV_LORA, 2 * MLA_HEADS * HEAD_DIM)
    rope_g_pad = jnp.pad(mla_rope_g.astype(F32), ((0, 0), (0, 0), (0, HEAD_DIM - MLA_ROPE)))
    cache_krope_pad = jnp.pad(cache_mla_krope, ((0, 0), (0, 0), (0, 0), (0, HEAD_DIM - MLA_ROPE)))
    cache_ckv_rows = cache_mla_ckv.reshape(DEC_BATCH * L * PAST_LEN, MLA_KV_LORA)
    c_na_k = cache_na_k.reshape(DEC_BATCH, L, PAST_LEN, NA_HEADS * HEAD_DIM)
    c_na_v = cache_na_v.reshape(DEC_BATCH, L, PAST_LEN, NA_HEADS * HEAD_DIM)
    c_gqa_k = cache_gqa_k.reshape(DEC_BATCH, L, PAST_LEN, GQA_KV_HEADS * HEAD_DIM)
    c_gqa_v = cache_gqa_v.reshape(DEC_BATCH, L, PAST_LEN, GQA_KV_HEADS * HEAD_DIM)

    wr = jnp.pad(moe_w_router.astype(F32), ((0, 0), (0, 0), (0, LANE - N_EXPERTS)))
    wr_hi = wr.astype(BF16)
    wr_lo = (wr - wr_hi.astype(F32)).astype(BF16)
    br = jnp.pad(moe_b_router.astype(F32), ((0, 0), (0, LANE - N_EXPERTS))).reshape(L, 1, LANE)
    b_gate4 = moe_b_gate.astype(F32).reshape(L, N_EXPERTS, 1, D_FF)
    b_up4 = moe_b_up.astype(F32).reshape(L, N_EXPERTS, 1, D_FF)
    b_down4 = moe_b_down.astype(F32).reshape(L, N_EXPERTS, 1, D)

    sm_scale = HEAD_DIM ** -0.5
    mla_scale = (MLA_NOPE + MLA_ROPE) ** -0.5
    g1 = lambda v: v.astype(F32).reshape(1, -1)

    def ctx_rows(width, cb):
        return (SEQ, width), lambda b, qi: (b, cb)

    def lat_rows(width, cb):
        return (DEC_SEQ, width), lambda b, qi: (ctx_blocks_1k + b, cb)

    def cache_rows(width, layer):
        return (None, None, PAST_LEN, width), lambda b, qi: (b, layer, 0, 0)

    def attend(q_parts, k_parts, v, ctx_src, *, k_heads, v_heads, bias=None, name):
        def src(rows):
            return ([(a, *rows(w, 0)) for a, w in k_parts], (v[0], *rows(v[1], v[2])))
        common = dict(heads=NA_HEADS, k_heads=k_heads, v_heads=v_heads)
        o_ctx = _attention(q_parts, [src(ctx_rows)], n_batch=BATCH, seq=SEQ, tq=SEQ, q_row0=0, name=name + "_ctx", **common)
        o_lat = _attention(q_parts, [src(lat_rows), ctx_src], n_batch=DEC_BATCH, seq=DEC_SEQ, tq=256, q_row0=N_CTX,
                           bias=bias, name=name + "_lat", **common)
        return jnp.concatenate([o_ctx, o_lat], axis=0)

    new = {k: [] for k in ("na_k", "na_v", "ckv", "krope", "gqa_k", "gqa_v", "s5")}
    for l in range(L):
        h = _norm1(x, norm_g4, mod4, l)
        win = functools.partial(_matmul, h, w_in, w_lead=(l,), tm=tm)
        z1 = win(col0=0, ncols=COL_BKR, tn=512, out_dtype=F32, name="w_in_a")
        zkr = win(col0=COL_BKR, ncols=LANE, tn=LANE, out_dtype=F32, name="w_in_kr")
        z2 = win(col0=COL_C, ncols=COL_GATE - COL_C, tn=512, out_dtype=F32, name="w_in_c")
        zg = win(col0=COL_GATE, ncols=N_BRANCHES * D, tn=512, out_dtype=BF16, name="w_in_gate")

        qa = _headnorm(z1, g1(na_qk_g[l, 0]) * sm_scale, col_block=0, heads=NA_HEADS, name="na_q")
        ka, ka32 = _headnorm(z1, g1(na_qk_g[l, 1]), col_block=1, heads=NA_HEADS, f32_width=HEAD_DIM, name="na_k")
        o_a = attend([(qa, 0)], [(ka, 1024)], (z1, 1024, 2),
                     ([(c_na_k, *cache_rows(1024, l))], (c_na_v, *cache_rows(1024, l))),
                     k_heads=(NA_HEADS,), v_heads=NA_HEADS,
                     bias=(na_bias, (None, NA_HEADS, 256, DEC_SEQ), lambda b, qi, l=l: (l, 0, qi, 0)), name="na")
        new["na_k"].append(ka32[:N_CTX])
        new["na_v"].append(z1[:N_CTX, 2048:3072])

        ql = _headnorm(z1, g1(mla_q_lora_g[l]), col_block=3, heads=1, dh=MLA_Q_LORA, name="mla_ql")
        mq = _matmul(ql, wq_perm, w_lead=(l,), tm=tm, tn=512, out_dtype=F32, name="mla_q_up")
        mq_nope = _headnorm(mq, g1(mla_nope_g[l, 0]) * mla_scale, col_block=0, heads=MLA_HEADS, name="mla_q_nope")
        mq_rope = _headnorm(mq, g1(rope_g_pad[l, 0]) * mla_scale, col_block=1, heads=MLA_HEADS, dh_eff=MLA_ROPE,
                            rope=mla_tabs, rope_half=mla_half, name="mla_q_rope")
        ckv, ckv32 = _headnorm(z1, g1(mla_kv_lora_g[l]), col_block=COL_BKR // MLA_KV_LORA - 1, heads=1, dh=MLA_KV_LORA,
                               f32_width=MLA_KV_LORA, name="mla_ckv")
        kr, kr32 = _headnorm(zkr, g1(rope_g_pad[l, 1]), col_block=0, heads=1, dh_eff=MLA_ROPE, rope=mla_tabs,
                             rope_half=mla_half, f32_width=MLA_ROPE, name="mla_krope")
        kv = _matmul(ckv, wkv_perm, w_lead=(l,), tm=tm, tn=512, out_dtype=BF16, name="mla_kv_up")
        k_nope = _headnorm(kv, g1(mla_nope_g[l, 1]), col_block=0, heads=MLA_HEADS, name="mla_k_nope")
        kvc = _matmul(cache_ckv_rows, wkv_perm, w_lead=(l,), m_rows=DEC_BATCH * PAST_LEN, tm=PAST_LEN, tn=512,
                      a_index_map=lambda n, m, l=l: (m * L + l, 0), out_dtype=BF16, name="mla_kv_up_cache")
        kc_nope = _headnorm(kvc, g1(mla_nope_g[l, 1]), col_block=0, heads=MLA_HEADS, name="mla_kc_nope")
        ctx_src = ([(kc_nope, (PAST_LEN, 1024), lambda b, qi: (b, 0)), (cache_krope_pad, *cache_rows(HEAD_DIM, l))],
                   (kvc, (PAST_LEN, 1024), lambda b, qi: (b, 1)))
        o_b = attend([(mq_nope, 0), (mq_rope, 0)], [(k_nope, 1024), (kr, HEAD_DIM)], (kv, 1024, 1), ctx_src,
                     k_heads=(MLA_HEADS, 1), v_heads=MLA_HEADS, name="mla")
        new["ckv"].append(ckv32[:N_CTX])
        new["krope"].append(kr32[:N_CTX])

        gq = _headnorm(z2, g1(gqa_qk_g[l, 0]) * sm_scale, col_block=0, heads=GQA_Q_HEADS, rope=gqa_tabs,
                       rope_half=gqa_half, name="gqa_q")
        gk, gk32 = _headnorm(z2, g1(gqa_qk_g[l, 1]), col_block=4, heads=GQA_KV_HEADS, rope=gqa_tabs, rope_half=gqa_half,
                             f32_width=HEAD_DIM, name="gqa_k")
        o_c = attend([(gq, 0)], [(gk, 256)], (z2, 256, 5),
                     ([(c_gqa_k, *cache_rows(256, l))], (c_gqa_v, *cache_rows(256, l))),
                     k_heads=(GQA_KV_HEADS,), v_heads=GQA_KV_HEADS, name="gqa")
        new["gqa_k"].append(gk32[:N_CTX])
        new["gqa_v"].append(z2[:N_CTX, 1280:1536])

        u_cb = 1536 // 256
        bu = _matmul(z2, s5_wb, w_lead=(l,), ncols=2 * S5_GC * S5_CHUNK_COLS, tm=tm, tn=S5_CHUNK_COLS, out_dtype=F32,
                     k=256, a_index_map=lambda n, m: (m, u_cb + n % S5_GC), name="s5_bu")
        bu3 = bu.reshape(N_TOK, 2 * S5_HALF_ROWS, LANE)
        hf_c, hb_c, fin_c = _s5_scan(bu3, lam_tiles, h0_ctx, layer=l, n_batch=BATCH, seq=SEQ, row0=0)
        hf_l, hb_l, _ = _s5_scan(bu3, lam_tiles, h0_lat[:, l], layer=l, n_batch=DEC_BATCH, seq=DEC_SEQ, row0=N_CTX)
        new["s5"].append(fin_c)

        def s5_out(hf, hb, row_block0, n_rows):
            flat = lambda t: t.reshape(n_rows, S5_HALF_ROWS * LANE)
            return _multi_dot(
                [(flat(hf), (tm, S5_CHUNK_COLS), lambda n, m: (m, n)), (flat(hb), (tm, S5_CHUNK_COLS), lambda n, m: (m, n))],
                s5_wc, pl.BlockSpec((None, 2, None, S5_CHUNK_COLS, 256), lambda n, m: (l, 0, n, 0, 0)),
                m_rows=n_rows, ncols=BRANCH_WIDTH, tm=tm, tn=256, out_dtype=BF16,
                extra=[(z2, (tm, 256), lambda n, m: (row_block0 + m, u_cb + n)),
                       (s5_d3, (None, 1, 256), lambda n, m: (l, 0, n))],
                epilogue=lambda acc, u, dvec: jax.nn.gelu(acc + dvec * u), name="s5_out")

        yy = jnp.concatenate([s5_out(hf_c, hb_c, 0, N_CTX), s5_out(hf_l, hb_l, N_CTX // tm, N_LAT)], axis=0)
        o_d = _matmul(yy, s5_w_glu, w_lead=(l,), tm=tm, tn=512, out_dtype=BF16,
                      extra=[(yy, (tm, 512), lambda n, m: (m, n))],
                      epilogue=lambda acc, y: y.astype(F32) * jax.nn.sigmoid(acc), name="s5_glu")

        gate_blocks = D // 512
        merged = _multi_dot(
            [(o, (tm, BRANCH_WIDTH), lambda n, m: (m, 0)) for o in (o_a, o_b, o_c, o_d)],
            w_branch, pl.BlockSpec((None, N_BRANCHES, BRANCH_WIDTH, 512), lambda n, m: (l, 0, 0, n)),
            m_rows=N_TOK, ncols=D, tm=tm, tn=512, out_dtype=BF16,
            gates=[(zg, (tm, 512), lambda n, m, i=i: (m, i * gate_blocks + n)) for i in range(N_BRANCHES)],
            name="branch_merge")
        x = _matmul(merged, w_out, w_lead=(l,), tm=tm, tn=512, out_dtype=F32,
                    extra=[(x, (tm, 512), lambda n, m: (m, n)),
                           (mod4, (None, None, 1, 512), lambda n, m: (l, _mod_row(m * tm), 0, 2 * gate_blocks + n))],
                    epilogue=lambda acc, xv, g: xv + g * acc, name="w_out")

        hp, e_sel, p_sel = _norm2_router(x, norm_g4, mod4, wr_hi, wr_lo, br, l)
        slot, slot_tok, tile_e, n_valid = _moe_routing(e_sel)
        xs = _moe_gather(slot_tok, n_valid, hp)
        ys = _moe_experts(tile_e, n_valid, xs, moe_w_gate, b_gate4, moe_w_up, b_up4, moe_w_down, b_down4, l)
        x = _moe_combine(slot, ys, x, p_sel, mod4, l)

    y_prompt = x[:N_CTX].reshape(BATCH, SEQ, D).astype(x_prompt.dtype)
    y_sample = x[N_CTX:].reshape(DEC_BATCH, DEC_SEQ, D).astype(x_sample.dtype)
    stack = lambda key, shape: jnp.stack([t.reshape((BATCH, SEQ) + shape) for t in new[key]], axis=1)
    fin = jnp.stack(new["s5"], axis=1).reshape(BATCH * L, 2 * S5_HALF_ROWS, LANE)
    s5_re, s5_im = _s5_split_tiles(fin)
    s5_shape = (BATCH, L, 2, SSM_GROUPS, SSM_STATE)
    return (y_prompt, y_sample,
            stack("na_k", (NA_HEADS, HEAD_DIM)), stack("na_v", (NA_HEADS, HEAD_DIM)),
            stack("ckv", (MLA_KV_LORA,)), stack("krope", (MLA_ROPE,)),
            stack("gqa_k", (GQA_KV_HEADS, HEAD_DIM)), stack("gqa_v", (GQA_KV_HEADS, HEAD_DIM)),
            s5_re.reshape(s5_shape), s5_im.reshape(s5_shape))
```

```python
import functools
import math

import jax
import jax.numpy as jnp
import numpy as np
from jax import lax
from jax.experimental import pallas as pl
from jax.experimental.pallas import tpu as pltpu

F32 = jnp.float32
BF16 = jnp.bfloat16
U32 = jnp.uint32
I32 = jnp.int32

D_MODEL = 4096
BATCH = 32
SEQ = 256
DEPTH = 4
DEC_BATCH = 4
DEC_SEQ = 1024
PAST_LEN = 512
GRID_W = 64
HEAD_DIM = 128
N_BRANCHES = 4
BRANCH_WIDTH = D_MODEL // N_BRANCHES
ROPE_THETA = 10000.0
NEG_INF = -1e30
EPS = 1e-6
NA_HEADS = 8
NA_WIN_H = 8
NA_WIN_W = 16
MLA_HEADS = 8
MLA_NOPE = 128
MLA_ROPE = 64
MLA_Q_LORA = 1024
MLA_KV_LORA = 512
GQA_Q_HEADS = 8
GQA_KV_HEADS = 2
SSM_GROUP_CH = 16
SSM_GROUPS = 64
SSM_STATE = 64
N_EXPERTS = 32
TOP_K = 4
D_FF = 1024
SWIGLU_LIMIT = 7.0
SWIGLU_ALPHA = 1.702

N_CTX = BATCH * SEQ
N_LAT = DEC_BATCH * DEC_SEQ
N_TOK = N_CTX + N_LAT
MOD_ROWS = 8

COL_BKR = 3072 + MLA_Q_LORA + MLA_KV_LORA
COL_C = COL_BKR + MLA_ROPE
COL_GATE = COL_C + 1024 + 256 + 256 + 1024
IN_WIDTH = COL_GATE + N_BRANCHES * D_MODEL

LANE = 128
VMEM_LIMIT = 52 * 1024 * 1024

S5_GC = 4
S5_CHUNK_COLS = 2048

MOE_TM = 512
MOE_TILES = N_TOK * TOP_K // MOE_TM + N_EXPERTS
MOE_SLOTS = MOE_TILES * MOE_TM
DMA_UNROLL = 8


def _cparams(n_axes, vmem=VMEM_LIMIT):
    return pltpu.CompilerParams(dimension_semantics=("arbitrary",) * n_axes, vmem_limit_bytes=vmem)


def _mod_row(row_start):
    return jnp.where(row_start < N_CTX, 0, 1 + (row_start - N_CTX) // DEC_SEQ)


def _cast_weight_tile(dst_ref, w_ref, w_next_ref, lane_shift, rows_per_step=256):
    k = w_ref.shape[0]
    rows = min(rows_per_step, k)

    def body(i, _):
        r = pl.multiple_of(i * rows, rows)
        w = w_ref[pl.ds(r, rows), :]
        if lane_shift:
            w = jnp.concatenate([w[:, lane_shift:], w_next_ref[pl.ds(r, rows), :lane_shift]], axis=1)
        dst_ref[pl.ds(r, rows), :] = w.astype(BF16)
        return 0

    lax.fori_loop(0, k // rows, body, 0)


def _store_lane_tiles(o_ref, val):
    for r in range(o_ref.shape[1]):
        o_ref[:, r, :] = val[:, r * LANE:(r + 1) * LANE].astype(o_ref.dtype)


def _load_lane_tiles(x_ref):
    return jnp.concatenate([x_ref[:, r, :] for r in range(x_ref.shape[1])], axis=1)


def _mm_kernel(*refs, lane_shift, n_extra, epilogue):
    a_ref, w_ref = refs[0], refs[1]
    pos = 2
    w_next_ref = None
    if lane_shift:
        w_next_ref = refs[pos]
        pos += 1
    extra = refs[pos:pos + n_extra]
    o_ref = refs[pos + n_extra]
    wb_ref = refs[pos + n_extra + 1]

    @pl.when(pl.program_id(1) == 0)
    def _():
        _cast_weight_tile(wb_ref, w_ref, w_next_ref, lane_shift)

    acc = jnp.dot(a_ref[...].astype(BF16), wb_ref[...], preferred_element_type=F32)
    res = epilogue(acc, *[e[...] for e in extra])
    if len(o_ref.shape) == 3:
        _store_lane_tiles(o_ref, res)
    else:
        o_ref[...] = res.astype(o_ref.dtype)


def _matmul(a, w, *, w_lead=(), col0=0, ncols=None, tm, tn, out_dtype, a_index_map=None, m_rows=None, k=None,
            extra=(), epilogue=None, out_tiled=False, name="matmul"):
    m_rows = m_rows or a.shape[0]
    a_index_map = a_index_map or (lambda n, m: (m, 0))
    k = k or w.shape[-2]
    ncols = ncols or w.shape[-1]
    lane_shift = col0 % LANE
    assert lane_shift in (0, 64) and m_rows % tm == 0 and ncols % tn == 0 and (col0 - lane_shift) % tn == 0
    nb0 = (col0 - lane_shift) // tn
    lead = tuple(w_lead)
    nlead = (None,) * len(lead)
    in_specs = [
        pl.BlockSpec((tm, k), a_index_map),
        pl.BlockSpec(nlead + (k, tn), lambda n, m: lead + (0, nb0 + n)),
    ]
    args = [a, w]
    if lane_shift:
        in_specs.append(pl.BlockSpec(nlead + (k, LANE), lambda n, m: lead + (0, (nb0 + n + 1) * (tn // LANE))))
        args.append(w)
    for arr, bshape, imap in extra:
        in_specs.append(pl.BlockSpec(bshape, imap))
        args.append(arr)
    epi = epilogue or (lambda acc: acc)
    if out_tiled:
        out_spec = pl.BlockSpec((tm, tn // LANE, LANE), lambda n, m: (m, n, 0))
        out_shape = jax.ShapeDtypeStruct((m_rows, ncols // LANE, LANE), out_dtype)
    else:
        out_spec = pl.BlockSpec((tm, tn), lambda n, m: (m, n))
        out_shape = jax.ShapeDtypeStruct((m_rows, ncols), out_dtype)
    return pl.pallas_call(
        functools.partial(_mm_kernel, lane_shift=lane_shift, n_extra=len(extra), epilogue=epi),
        grid=(ncols // tn, m_rows // tm),
        in_specs=in_specs,
        out_specs=out_spec,
        out_shape=out_shape,
        scratch_shapes=[pltpu.VMEM((k, tn), BF16)],
        compiler_params=_cparams(2),
        name=name,
    )(*args)


def _ada_kernel(c_ref, w_ref, b_ref, o_ref, wb_ref):
    _cast_weight_tile(wb_ref, w_ref, None, 0)
    c = c_ref[...]
    s = (c * jax.nn.sigmoid(c)).astype(BF16)
    o_ref[...] = jnp.dot(s, wb_ref[...], preferred_element_type=F32) + b_ref[...]


def _ada_modulation(cond, w_ada, b_ada, tn=512):
    depth, d, n = w_ada.shape
    rows = cond.shape[0]
    return pl.pallas_call(
        _ada_kernel,
        grid=(depth, n // tn),
        in_specs=[
            pl.BlockSpec((rows, d), lambda l, j: (0, 0)),
            pl.BlockSpec((None, d, tn), lambda l, j: (l, 0, j)),
            pl.BlockSpec((None, 1, tn), lambda l, j: (l, 0, j)),
        ],
        out_specs=pl.BlockSpec((None, rows, tn), lambda l, j: (l, 0, j)),
        out_shape=jax.ShapeDtypeStruct((depth, rows, n), F32),
        scratch_shapes=[pltpu.VMEM((d, tn), BF16)],
        compiler_params=_cparams(2),
        name="ada_modulation",
    )(cond, w_ada, b_ada.reshape(depth, 1, n))


def _pack_bf16_pair(lo, hi):
    lo_bits = lax.bitcast_convert_type(lo.astype(BF16).astype(F32), U32) >> 16
    hi_bits = lax.bitcast_convert_type(hi.astype(BF16).astype(F32), U32) & jnp.uint32(0xFFFF0000)
    return lo_bits | hi_bits


def _unpack_bf16_pair(word):
    lo = lax.bitcast_convert_type(word << 16, F32)
    hi = lax.bitcast_convert_type(word & jnp.uint32(0xFFFF0000), F32)
    return lo, hi


def _modulated_norm(x, g, scale, shift):
    y = x * lax.rsqrt(jnp.mean(x * x, axis=-1, keepdims=True) + EPS)
    return (y * g) * (1.0 + scale) + shift


def _norm_kernel(x_ref, g_ref, shift_ref, scale_ref, o_ref):
    o_ref[...] = _modulated_norm(x_ref[...], g_ref[...], scale_ref[...], shift_ref[...]).astype(o_ref.dtype)


def _split_hi_lo(v):
    hi = v.astype(BF16)
    return hi, (v - hi.astype(F32)).astype(BF16)


def _norm_router_kernel(x_ref, g_ref, shift_ref, scale_ref, wr_hi_ref, wr_lo_ref, br_ref,
                        hp_ref, e_ref, p_ref):
    h = _modulated_norm(x_ref[...], g_ref[...], scale_ref[...], shift_ref[...])
    half = h.shape[1] // 2
    _store_lane_tiles(hp_ref, _pack_bf16_pair(h[:, :half], h[:, half:]))
    h_hi, h_lo = _split_hi_lo(h)
    logits = (jnp.dot(h_hi, wr_hi_ref[...], preferred_element_type=F32)
              + jnp.dot(h_hi, wr_lo_ref[...], preferred_element_type=F32)
              + jnp.dot(h_lo, wr_hi_ref[...], preferred_element_type=F32)) + br_ref[...]
    lane = lax.broadcasted_iota(I32, logits.shape, 1)
    logits = jnp.where(lane < N_EXPERTS, logits, -jnp.inf)
    e_out = jnp.zeros(logits.shape, I32)
    p_out = jnp.zeros(logits.shape, F32)
    top = None
    denom = None
    for kk in range(TOP_K):
        m = jnp.max(logits, axis=-1, keepdims=True)
        idx = jnp.min(jnp.where(logits == m, lane, LANE), axis=-1, keepdims=True)
        if kk == 0:
            top = m
        p = jnp.exp(m - top)
        denom = p if kk == 0 else denom + p
        e_out = jnp.where(lane == kk, idx, e_out)
        p_out = jnp.where(lane == kk, p, p_out)
        logits = jnp.where(lane == idx, -jnp.inf, logits)
    e_ref[...] = e_out
    p_ref[...] = p_out / denom


def _norm_specs(tm, d, layer, which):
    return [
        pl.BlockSpec((tm, d), lambda i: (i, 0)),
        pl.BlockSpec((None, None, 1, d), lambda i: (layer, which, 0, 0)),
        pl.BlockSpec((None, None, 1, d), lambda i: (layer, _mod_row(i * tm), 0, 3 * which)),
        pl.BlockSpec((None, None, 1, d), lambda i: (layer, _mod_row(i * tm), 0, 3 * which + 1)),
    ]


def _norm1(x, norm_g4, mod4, layer, tm=256):
    n, d = x.shape
    return pl.pallas_call(
        _norm_kernel,
        grid=(n // tm,),
        in_specs=_norm_specs(tm, d, layer, 0),
        out_specs=pl.BlockSpec((tm, d), lambda i: (i, 0)),
        out_shape=jax.ShapeDtypeStruct((n, d), BF16),
        compiler_params=_cparams(1),
        name="norm1",
    )(x, norm_g4, mod4, mod4)


def _norm2_router(x, norm_g4, mod4, wr_hi, wr_lo, br, layer, tm=256):
    n, d = x.shape
    const = lambda i: (layer, 0, 0)
    word_rows = d // 2 // LANE
    return pl.pallas_call(
        _norm_router_kernel,
        grid=(n // tm,),
        in_specs=_norm_specs(tm, d, layer, 1) + [
            pl.BlockSpec((None, d, LANE), const),
            pl.BlockSpec((None, d, LANE), const),
            pl.BlockSpec((None, 1, LANE), const),
        ],
        out_specs=[
            pl.BlockSpec((tm, word_rows, LANE), lambda i: (i, 0, 0)),
            pl.BlockSpec((tm, LANE), lambda i: (i, 0)),
            pl.BlockSpec((tm, LANE), lambda i: (i, 0)),
        ],
        out_shape=[
            jax.ShapeDtypeStruct((n, word_rows, LANE), U32),
            jax.ShapeDtypeStruct((n, LANE), I32),
            jax.ShapeDtypeStruct((n, LANE), F32),
        ],
        compiler_params=_cparams(1),
        name="norm2_router",
    )(x, norm_g4, mod4, mod4, wr_hi, wr_lo, br)


def _headnorm_kernel(*refs, heads, dh, dh_eff, rope_half, f32_width):
    x_ref, g_ref = refs[0], refs[1]
    pos = 2
    tabs = None
    if rope_half:
        tabs = [r[...] for r in refs[pos:pos + 3]]
        pos += 3
    ob_ref = refs[pos]
    of_ref = refs[pos + 1] if f32_width else None
    g = g_ref[...]
    for h in range(heads):
        x = x_ref[:, h * dh:(h + 1) * dh].astype(F32)
        if dh_eff < dh:
            lane = lax.broadcasted_iota(I32, x.shape, 1)
            x = jnp.where(lane < dh_eff, x, 0.0)
        y = x * lax.rsqrt(jnp.sum(x * x, axis=-1, keepdims=True) * (1.0 / dh_eff) + EPS) * g
        if rope_half:
            c, sa, sb = tabs
            y = y * c + pltpu.roll(y, rope_half, 1) * sa + pltpu.roll(y, dh - rope_half, 1) * sb
        ob_ref[:, h * dh:(h + 1) * dh] = y.astype(ob_ref.dtype)
        if of_ref is not None:
            of_ref[:, h * f32_width:(h + 1) * f32_width] = y[:, :f32_width]


def _headnorm(x, gain, *, col_block, heads, dh=HEAD_DIM, dh_eff=None, rope=None, rope_half=0,
              f32_width=0, tm=256, name="headnorm"):
    n = x.shape[0]
    w = heads * dh
    dh_eff = dh_eff or dh
    in_specs = [pl.BlockSpec((tm, w), lambda i: (i, col_block)), pl.BlockSpec((1, dh), lambda i: (0, 0))]
    args = [x, gain]
    if rope is not None:
        in_specs += [pl.BlockSpec((tm, dh), lambda i: (i, 0))] * 3
        args += list(rope)
    out_specs = [pl.BlockSpec((tm, w), lambda i: (i, 0))]
    out_shape = [jax.ShapeDtypeStruct((n, w), BF16)]
    if f32_width:
        out_specs.append(pl.BlockSpec((tm, heads * f32_width), lambda i: (i, 0)))
        out_shape.append(jax.ShapeDtypeStruct((n, heads * f32_width), F32))
    out = pl.pallas_call(
        functools.partial(_headnorm_kernel, heads=heads, dh=dh, dh_eff=dh_eff,
                          rope_half=rope_half if rope is not None else 0, f32_width=f32_width),
        grid=(n // tm,),
        in_specs=in_specs,
        out_specs=out_specs,
        out_shape=out_shape,
        compiler_params=_cparams(1),
        name=name,
    )(*args)
    return out if f32_width else out[0]


def _attn_kernel(*refs, heads, n_parts, k_heads, v_heads, n_src, has_bias):
    d = HEAD_DIM
    q_refs = refs[:n_parts]
    pos = n_parts
    srcs = []
    for _ in range(n_src):
        srcs.append((refs[pos:pos + n_parts], refs[pos + n_parts]))
        pos += n_parts + 1
    bias_ref = refs[pos] if has_bias else None
    o_ref = refs[-1]
    contract_last = (((1,), (1,)), ((), ()))
    for h in range(heads):
        scores = []
        for si, (k_refs, _) in enumerate(srcs):
            s = None
            for p in range(n_parts):
                kh = h // (heads // k_heads[p])
                q = q_refs[p][:, h * d:(h + 1) * d]
                k = k_refs[p][:, kh * d:(kh + 1) * d].astype(BF16)
                part = lax.dot_general(q, k, contract_last, preferred_element_type=F32)
                s = part if s is None else s + part
            if si == 0 and has_bias:
                s = s + bias_ref[h]
            scores.append(s)
        m = functools.reduce(jnp.maximum, [jnp.max(s, axis=-1, keepdims=True) for s in scores])
        probs = [jnp.exp(s - m) for s in scores]
        denom = functools.reduce(lambda a, b: a + b, [jnp.sum(p, axis=-1, keepdims=True) for p in probs])
        vh = h // (heads // v_heads)
        o = None
        for p, (_, v_ref) in zip(probs, srcs):
            v = v_ref[:, vh * d:(vh + 1) * d].astype(BF16)
            pv = jnp.dot(p.astype(BF16), v, preferred_element_type=F32)
            o = pv if o is None else o + pv
        o_ref[:, h * d:(h + 1) * d] = (o / denom).astype(o_ref.dtype)


def _attention(q_parts, sources, *, n_batch, seq, tq, q_row0, heads, k_heads, v_heads, bias=None, name="attention"):
    w = heads * HEAD_DIM
    nq = seq // tq
    qb0 = q_row0 // tq
    in_specs, args = [], []
    for arr, cb in q_parts:
        in_specs.append(pl.BlockSpec((tq, w), lambda b, qi, cb=cb: (qb0 + b * nq + qi, cb)))
        args.append(arr)
    for k_parts, v in sources:
        for arr, bshape, imap in list(k_parts) + [v]:
            in_specs.append(pl.BlockSpec(bshape, imap))
            args.append(arr)
    if bias is not None:
        in_specs.append(pl.BlockSpec(bias[1], bias[2]))
        args.append(bias[0])
    return pl.pallas_call(
        functools.partial(_attn_kernel, heads=heads, n_parts=len(q_parts), k_heads=k_heads, v_heads=v_heads,
                          n_src=len(sources), has_bias=bias is not None),
        grid=(n_batch, nq),
        in_specs=in_specs,
        out_specs=pl.BlockSpec((tq, w), lambda b, qi: (b * nq + qi, 0)),
        out_shape=jax.ShapeDtypeStruct((n_batch * seq, w), BF16),
        compiler_params=_cparams(2),
        name=name,
    )(*args)


S5_HALF_ROWS = 64


def _s5_step(h, bu, lam):
    h4 = h.reshape(S5_GC, 2, 8, LANE)
    b4 = bu.reshape(S5_GC, 2, 8, LANE)
    a4 = lam.reshape(S5_GC, 2, 8, LANE)
    hr, hi = h4[:, 0], h4[:, 1]
    ar, ai = a4[:, 0], a4[:, 1]
    nr = ar * hr - ai * hi + b4[:, 0]
    ni = ar * hi + ai * hr + b4[:, 1]
    return jnp.stack([nr, ni], axis=1).reshape(S5_HALF_ROWS, LANE)


def _s5_scan_kernel(buf_ref, bub_ref, lam_ref, h0_ref, hf_ref, hb_ref, fin_ref, state_ref, *, tc):
    j = pl.program_id(1)

    @pl.when(j == 0)
    def _():
        state_ref[...] = h0_ref[...]

    lam_f = lam_ref[:S5_HALF_ROWS, :]
    lam_b = lam_ref[S5_HALF_ROWS:, :]

    def body(i, carry):
        hf, hb = carry
        hf = _s5_step(hf, buf_ref[i], lam_f)
        hb = _s5_step(hb, bub_ref[tc - 1 - i], lam_b)
        hf_ref[i] = hf.astype(hf_ref.dtype)
        hb_ref[tc - 1 - i] = hb.astype(hb_ref.dtype)
        return hf, hb

    hf, hb = lax.fori_loop(0, tc, body, (state_ref[:S5_HALF_ROWS, :], state_ref[S5_HALF_ROWS:, :]), unroll=4)
    state_ref[:S5_HALF_ROWS, :] = hf
    state_ref[S5_HALF_ROWS:, :] = hb

    @pl.when(j == pl.num_programs(1) - 1)
    def _():
        fin_ref[...] = state_ref[...]


def _s5_scan(bu3, lam_tiles, h0, *, layer, n_batch, seq, row0, tc=64):
    nt = seq // tc
    rb0 = row0 // tc
    n = n_batch * seq
    return pl.pallas_call(
        functools.partial(_s5_scan_kernel, tc=tc),
        grid=(n_batch, nt),
        in_specs=[
            pl.BlockSpec((tc, S5_HALF_ROWS, LANE), lambda b, j: (rb0 + b * nt + j, 0, 0)),
            pl.BlockSpec((tc, S5_HALF_ROWS, LANE), lambda b, j: (rb0 + b * nt + (nt - 1 - j), 1, 0)),
            pl.BlockSpec((None, 2 * S5_HALF_ROWS, LANE), lambda b, j: (layer, 0, 0)),
            pl.BlockSpec((None, 2 * S5_HALF_ROWS, LANE), lambda b, j: (b, 0, 0)),
        ],
        out_specs=[
            pl.BlockSpec((tc, S5_HALF_ROWS, LANE), lambda b, j: (b * nt + j, 0, 0)),
            pl.BlockSpec((tc, S5_HALF_ROWS, LANE), lambda b, j: (b * nt + (nt - 1 - j), 0, 0)),
            pl.BlockSpec((None, 2 * S5_HALF_ROWS, LANE), lambda b, j: (b, 0, 0)),
        ],
        out_shape=[
            jax.ShapeDtypeStruct((n, S5_HALF_ROWS, LANE), BF16),
            jax.ShapeDtypeStruct((n, S5_HALF_ROWS, LANE), BF16),
            jax.ShapeDtypeStruct((n_batch, 2 * S5_HALF_ROWS, LANE), F32),
        ],
        scratch_shapes=[pltpu.VMEM((2 * S5_HALF_ROWS, LANE), F32)],
        compiler_params=_cparams(2),
        name="s5_scan",
    )(bu3, bu3, lam_tiles, h0)


def _multi_dot_kernel(*refs, n_terms, gated, n_extra, epilogue):
    a_refs = refs[:n_terms]
    w_ref = refs[n_terms]
    pos = n_terms + 1
    g_refs = refs[pos:pos + n_terms] if gated else None
    pos += n_terms if gated else 0
    extra = refs[pos:pos + n_extra]
    o_ref = refs[pos + n_extra]
    wb_ref = refs[pos + n_extra + 1]

    @pl.when(pl.program_id(1) == 0)
    def _():
        for t in range(n_terms):
            _cast_weight_tile(wb_ref.at[t], w_ref.at[t], None, 0)

    acc = None
    for t in range(n_terms):
        a = _load_lane_tiles(a_refs[t]) if len(a_refs[t].shape) == 3 else a_refs[t][...]
        part = jnp.dot(a.astype(BF16), wb_ref[t], preferred_element_type=F32)
        if gated:
            part = part * jax.nn.sigmoid(g_refs[t][...].astype(F32))
        acc = part if acc is None else acc + part
    o_ref[...] = epilogue(acc, *[e[...] for e in extra]).astype(o_ref.dtype)


def _multi_dot(a_terms, w, w_spec, *, m_rows, ncols, tm, tn, out_dtype, gates=None, extra=(), epilogue=None,
               name="multi_dot"):
    n_terms = len(a_terms)
    kdim = math.prod(a_terms[0][1][1:])
    in_specs = [pl.BlockSpec(bs, im) for _, bs, im in a_terms] + [w_spec]
    args = [a for a, _, _ in a_terms] + [w]
    if gates is not None:
        in_specs += [pl.BlockSpec(bs, im) for _, bs, im in gates]
        args += [g for g, _, _ in gates]
    for arr, bshape, imap in extra:
        in_specs.append(pl.BlockSpec(bshape, imap))
        args.append(arr)
    epi = epilogue or (lambda acc: acc)
    return pl.pallas_call(
        functools.partial(_multi_dot_kernel, n_terms=n_terms, gated=gates is not None, n_extra=len(extra), epilogue=epi),
        grid=(ncols // tn, m_rows // tm),
        in_specs=in_specs,
        out_specs=pl.BlockSpec((tm, tn), lambda n, m: (m, n)),
        out_shape=jax.ShapeDtypeStruct((m_rows, ncols), out_dtype),
        scratch_shapes=[pltpu.VMEM((n_terms, kdim, tn), BF16)],
        compiler_params=_cparams(2),
        name=name,
    )(*args)


def _row_gather(idx_of, src_hbm, dst_of, sem, n_rows):
    def copy(r):
        return pltpu.make_async_copy(src_hbm.at[pl.ds(idx_of(r), 1)], dst_of(r), sem)

    def start():
        lax.fori_loop(0, n_rows, lambda r, c: (copy(r).start(), c)[1], 0, unroll=DMA_UNROLL)

    def wait():
        lax.fori_loop(0, n_rows, lambda r, c: (copy(r).wait(), c)[1], 0, unroll=DMA_UNROLL)

    return start, wait


def _moe_gather_kernel(tok_ref, nv_ref, hp_hbm, o_ref, buf, sem, *, tm):
    t = pl.program_id(0)
    nv = nv_ref[0]

    def tile_ops(tile, slot):
        return _row_gather(lambda r: tok_ref[tile * tm + r], hp_hbm,
                           lambda r: buf.at[slot, pl.ds(r, 1)], sem.at[slot], tm)

    @pl.when((t == 0) & (nv > 0))
    def _():
        tile_ops(0, 0)[0]()

    @pl.when(t + 1 < nv)
    def _():
        tile_ops(t + 1, (t + 1) % 2)[0]()

    @pl.when(t < nv)
    def _():
        slot = t % 2
        tile_ops(t, slot)[1]()
        word_rows = buf.shape[2]
        half = word_rows * LANE
        for r in range(word_rows):
            lo, hi = _unpack_bf16_pair(buf[slot, :, r, :])
            o_ref[:, r * LANE:(r + 1) * LANE] = lo.astype(BF16)
            o_ref[:, half + r * LANE:half + (r + 1) * LANE] = hi.astype(BF16)

    @pl.when(t >= nv)
    def _():
        o_ref[...] = jnp.zeros_like(o_ref)


def _moe_gather(slot_tok, n_valid, hp, tm=MOE_TM):
    n_slots = slot_tok.shape[0]
    word_rows = hp.shape[1]
    half = word_rows * LANE
    return pl.pallas_call(
        functools.partial(_moe_gather_kernel, tm=tm),
        grid_spec=pltpu.PrefetchScalarGridSpec(
            num_scalar_prefetch=2,
            grid=(n_slots // tm,),
            in_specs=[pl.BlockSpec(memory_space=pl.ANY)],
            out_specs=pl.BlockSpec((tm, 2 * half), lambda t, tok, nv: (t, 0)),
            scratch_shapes=[pltpu.VMEM((2, tm, word_rows, LANE), U32), pltpu.SemaphoreType.DMA((2,))],
        ),
        out_shape=jax.ShapeDtypeStruct((n_slots, 2 * half), BF16),
        compiler_params=_cparams(1),
        name="moe_gather",
    )(slot_tok, n_valid, hp)


def _moe_tile_flags(te_ref, nv_ref, t):
    valid = t < nv_ref[0]
    new_expert = (t == 0) | (te_ref[t] != te_ref[jnp.maximum(t - 1, 0)])
    return valid, valid & new_expert


def _moe_up_kernel(te_ref, nv_ref, xs_ref, wg_ref, wu_ref, bg_ref, bu_ref, o_ref, wb_ref):
    valid, recast = _moe_tile_flags(te_ref, nv_ref, pl.program_id(1))

    @pl.when(recast)
    def _():
        _cast_weight_tile(wb_ref.at[0], wg_ref, None, 0)
        _cast_weight_tile(wb_ref.at[1], wu_ref, None, 0)

    @pl.when(valid)
    def _():
        x = xs_ref[...]
        gate = jnp.minimum(jnp.dot(x, wb_ref[0], preferred_element_type=F32) + bg_ref[...], SWIGLU_LIMIT)
        up = jnp.clip(jnp.dot(x, wb_ref[1], preferred_element_type=F32) + bu_ref[...], -SWIGLU_LIMIT, SWIGLU_LIMIT)
        o_ref[...] = (gate * jax.nn.sigmoid(SWIGLU_ALPHA * gate) * (up + 1.0)).astype(o_ref.dtype)

    @pl.when(jnp.logical_not(valid))
    def _():
        o_ref[...] = jnp.zeros_like(o_ref)


def _moe_down_kernel(te_ref, nv_ref, act_ref, wd_ref, bd_ref, o_ref, wb_ref):
    valid, recast = _moe_tile_flags(te_ref, nv_ref, pl.program_id(1))

    @pl.when(recast)
    def _():
        _cast_weight_tile(wb_ref, wd_ref, None, 0)

    @pl.when(valid)
    def _():
        y = jnp.dot(act_ref[...], wb_ref[...], preferred_element_type=F32) + bd_ref[...]
        half = y.shape[1] // 2
        _store_lane_tiles(o_ref, _pack_bf16_pair(y[:, :half], y[:, half:]))

    @pl.when(jnp.logical_not(valid))
    def _():
        o_ref[...] = jnp.zeros_like(o_ref)


def _moe_experts(tile_e, n_valid, xs, w_gate, b_gate, w_up, b_up, w_down, b_down, layer, tm=MOE_TM, tf=256, tn=2048):
    n_slots, d = xs.shape
    d_ff = w_gate.shape[-1]
    n_tiles = n_slots // tm
    row = lambda t, nv: jnp.minimum(t, jnp.maximum(nv[0] - 1, 0))
    act = pl.pallas_call(
        _moe_up_kernel,
        grid_spec=pltpu.PrefetchScalarGridSpec(
            num_scalar_prefetch=2,
            grid=(d_ff // tf, n_tiles),
            in_specs=[
                pl.BlockSpec((tm, d), lambda f, t, te, nv: (row(t, nv), 0)),
                pl.BlockSpec((None, None, d, tf), lambda f, t, te, nv: (layer, te[t], 0, f)),
                pl.BlockSpec((None, None, d, tf), lambda f, t, te, nv: (layer, te[t], 0, f)),
                pl.BlockSpec((None, None, 1, tf), lambda f, t, te, nv: (layer, te[t], 0, f)),
                pl.BlockSpec((None, None, 1, tf), lambda f, t, te, nv: (layer, te[t], 0, f)),
            ],
            out_specs=pl.BlockSpec((tm, tf), lambda f, t, te, nv: (t, f)),
            scratch_shapes=[pltpu.VMEM((2, d, tf), BF16)],
        ),
        out_shape=jax.ShapeDtypeStruct((n_slots, d_ff), BF16),
        compiler_params=_cparams(2),
        name="moe_up",
    )(tile_e, n_valid, xs, w_gate, w_up, b_gate, b_up)
    return pl.pallas_call(
        _moe_down_kernel,
        grid_spec=pltpu.PrefetchScalarGridSpec(
            num_scalar_prefetch=2,
            grid=(d // tn, n_tiles),
            in_specs=[
                pl.BlockSpec((tm, d_ff), lambda n, t, te, nv: (row(t, nv), 0)),
                pl.BlockSpec((None, None, d_ff, tn), lambda n, t, te, nv: (layer, te[t], 0, n)),
                pl.BlockSpec((None, None, 1, tn), lambda n, t, te, nv: (layer, te[t], 0, n)),
            ],
            out_specs=pl.BlockSpec((tm, tn // 2 // LANE, LANE), lambda n, t, te, nv: (t, n, 0)),
            scratch_shapes=[pltpu.VMEM((d_ff, tn), BF16)],
        ),
        out_shape=jax.ShapeDtypeStruct((n_slots, d // 2 // LANE, LANE), U32),
        compiler_params=_cparams(2),
        name="moe_down",
    )(tile_e, n_valid, act, w_down, b_down)


def _moe_combine_kernel(slot_ref, ys_hbm, x_ref, p_ref, gate_ref, o_ref, buf, sem, *, tt, tn):
    i = pl.program_id(0)
    n_steps = pl.num_programs(0)
    rows = tt * TOP_K

    def tile_ops(tile, slot):
        return _row_gather(lambda r: slot_ref[tile * rows + r], ys_hbm,
                           lambda r: buf.at[slot, r & (TOP_K - 1), pl.ds(lax.shift_right_logical(r, TOP_K.bit_length() - 1), 1)],
                           sem.at[slot], rows)

    @pl.when(i == 0)
    def _():
        tile_ops(0, 0)[0]()

    @pl.when(i + 1 < n_steps)
    def _():
        tile_ops(i + 1, (i + 1) % 2)[0]()

    slot = i % 2
    tile_ops(i, slot)[1]()
    p = p_ref[...]
    pk = [p[:, kk:kk + 1] for kk in range(TOP_K)]
    rows_per_group = tn // 2 // LANE
    for r in range(buf.shape[3]):
        lo = hi = None
        for kk in range(TOP_K):
            w_lo, w_hi = _unpack_bf16_pair(buf[slot, kk, :, r, :])
            lo = w_lo * pk[kk] if lo is None else lo + w_lo * pk[kk]
            hi = w_hi * pk[kk] if hi is None else hi + w_hi * pk[kk]
        c_lo = (r // rows_per_group) * tn + (r % rows_per_group) * LANE
        c_hi = c_lo + tn // 2
        o_ref[:, c_lo:c_lo + LANE] = x_ref[:, c_lo:c_lo + LANE] + gate_ref[:, c_lo:c_lo + LANE] * lo
        o_ref[:, c_hi:c_hi + LANE] = x_ref[:, c_hi:c_hi + LANE] + gate_ref[:, c_hi:c_hi + LANE] * hi


def _moe_combine(slots, ys, x, p_sel, mod4, layer, tt=128, tn=2048):
    n, d = x.shape
    return pl.pallas_call(
        functools.partial(_moe_combine_kernel, tt=tt, tn=tn),
        grid_spec=pltpu.PrefetchScalarGridSpec(
            num_scalar_prefetch=1,
            grid=(n // tt,),
            in_specs=[
                pl.BlockSpec(memory_space=pl.ANY),
                pl.BlockSpec((tt, d), lambda i, s: (i, 0)),
                pl.BlockSpec((tt, LANE), lambda i, s: (i, 0)),
                pl.BlockSpec((None, None, 1, d), lambda i, s: (layer, _mod_row(i * tt), 0, 5)),
            ],
            out_specs=pl.BlockSpec((tt, d), lambda i, s: (i, 0)),
            scratch_shapes=[pltpu.VMEM((2, TOP_K, tt, d // 2 // LANE, LANE), U32), pltpu.SemaphoreType.DMA((2,))],
        ),
        out_shape=jax.ShapeDtypeStruct((n, d), F32),
        compiler_params=_cparams(1),
        name="moe_combine",
    )(slots, ys, x, p_sel, mod4)


def _rope_tables(rot_dim):
    t = jnp.arange(DEC_SEQ)
    n_freq = rot_dim // 4
    freqs = ROPE_THETA ** (-jnp.arange(n_freq, dtype=F32) / n_freq)
    ang = jnp.concatenate([(t // GRID_W).astype(F32)[:, None] * freqs,
                           (t % GRID_W).astype(F32)[:, None] * freqs], axis=-1)
    cos, sin = jnp.cos(ang), jnp.sin(ang)
    half = rot_dim // 2
    pad = HEAD_DIM - rot_dim
    zeros = jnp.zeros_like(sin)

    def full(lat, ctx_row):
        lat = jnp.pad(lat, ((0, 0), (0, pad)))
        ctx = jnp.broadcast_to(jnp.pad(ctx_row, (0, pad))[None], (N_CTX, HEAD_DIM))
        return jnp.concatenate([ctx, jnp.tile(lat, (DEC_BATCH, 1))], axis=0)

    one = jnp.ones((rot_dim,), F32)
    zero = jnp.zeros((rot_dim,), F32)
    c = full(jnp.concatenate([cos, cos], axis=-1), one)
    sa = full(jnp.concatenate([zeros, sin], axis=-1), zero)
    sb = full(jnp.concatenate([-sin, zeros], axis=-1), zero)
    return (c, sa, sb), half


def _na_bias(na_rpb):
    rows = DEC_SEQ // GRID_W
    wh = min(NA_WIN_H, rows)
    r = np.arange(rows)
    row_lo = np.clip(r - wh // 2, 0, rows - wh)
    row_ok = (r[None, :] >= row_lo[:, None]) & (r[None, :] < row_lo[:, None] + wh)
    d_row = np.clip(r[None, :] - r[:, None] + (NA_WIN_H - 1), 0, 2 * NA_WIN_H - 2)
    col = np.arange(GRID_W)
    col_lo = np.clip(col - NA_WIN_W // 2, 0, GRID_W - NA_WIN_W)
    col_ok = (col[None, :] >= col_lo[:, None]) & (col[None, :] < col_lo[:, None] + NA_WIN_W)
    d_col = np.clip(col[None, :] - col[:, None], -(NA_WIN_W - 1), NA_WIN_W - 1) + (NA_WIN_W - 1)
    sel_row = (d_row[:, :, None] == np.arange(2 * NA_WIN_H - 1)).astype(np.float32)
    sel_col = (d_col[:, :, None] == np.arange(2 * NA_WIN_W - 1)).astype(np.float32)
    rpb = na_rpb.astype(F32)
    t1 = jnp.einsum('lhad,rsa->lhrsd', rpb, sel_row, precision=lax.Precision.HIGHEST)
    t2 = jnp.einsum('lhrsd,qkd->lhrqsk', t1, sel_col, precision=lax.Precision.HIGHEST)
    ok = row_ok[:, None, :, None] & col_ok[None, :, None, :]
    bias = jnp.where(ok[None, None], t2, NEG_INF)
    return bias.reshape(DEPTH, NA_HEADS, DEC_SEQ, DEC_SEQ)


def _s5_params(lam_re, lam_im, log_step, b_re, b_im, c_re, c_im):
    step = jnp.exp(log_step.astype(F32))[..., None]
    lr, li = lam_re.astype(F32), lam_im.astype(F32)
    mag = jnp.exp(lr * step)
    ar, ai = mag * jnp.cos(li * step), mag * jnp.sin(li * step)
    den = lr * lr + li * li
    kr = ((ar - 1.0) * lr + ai * li) / den
    ki = (ai * lr - (ar - 1.0) * li) / den
    bbr = kr[..., None] * b_re - ki[..., None] * b_im
    bbi = kr[..., None] * b_im + ki[..., None] * b_re
    gl = SSM_GROUPS // S5_GC
    eye = jnp.eye(gl, dtype=F32)

    def tiles(re, im):
        v = jnp.stack([re, im], axis=3)
        v = v.reshape(DEPTH, 2, S5_GC, gl, 2, SSM_STATE).transpose(0, 1, 2, 4, 3, 5)
        return v.reshape(DEPTH, 2 * S5_HALF_ROWS, LANE)

    lam_tiles = tiles(ar, ai)
    bb = jnp.stack([bbr, bbi], axis=-1)
    bb = bb.reshape(DEPTH, 2, S5_GC, gl, SSM_STATE, SSM_GROUP_CH, 2)
    wb = bb.transpose(0, 1, 2, 3, 5, 6, 4)[:, :, :, :, :, :, None, :] * eye[None, None, None, :, None, None, :, None]
    wb = wb.reshape(DEPTH, 2 * S5_GC, gl * SSM_GROUP_CH, S5_CHUNK_COLS)
    wb = wb.transpose(0, 2, 1, 3).reshape(DEPTH, gl * SSM_GROUP_CH, 2 * S5_GC * S5_CHUNK_COLS)
    cc = jnp.stack([c_re.astype(F32), -c_im.astype(F32)], axis=2)
    cc = cc.reshape(DEPTH, 2, 2, S5_GC, gl, SSM_GROUP_CH, SSM_STATE)
    wc = cc.transpose(0, 1, 3, 2, 4, 6, 5)[:, :, :, :, :, :, None, :] * eye[None, None, None, None, :, None, :, None]
    wc = wc.reshape(DEPTH, 2, S5_GC, S5_CHUNK_COLS, gl * SSM_GROUP_CH)
    return lam_tiles, wb.astype(BF16), wc.astype(BF16)


def _s5_state_tiles(re, im):
    gl = SSM_GROUPS // S5_GC
    v = jnp.stack([re.astype(F32), im.astype(F32)], axis=3)
    v = v.reshape(-1, 2, S5_GC, gl, 2, SSM_STATE).transpose(0, 1, 2, 4, 3, 5)
    return v.reshape(-1, 2 * S5_HALF_ROWS, LANE)


def _s5_split_tiles(tiles):
    gl = SSM_GROUPS // S5_GC
    v = tiles.reshape(-1, 2, S5_GC, 2, gl, SSM_STATE).transpose(0, 1, 3, 2, 4, 5)
    v = v.reshape(-1, 2, 2, SSM_GROUPS, SSM_STATE)
    return v[:, :, 0], v[:, :, 1]


def _moe_routing(e_sel, tm=MOE_TM):
    flat_e = e_sel[:, :TOP_K].reshape(-1)
    onehot = (flat_e[:, None] == jnp.arange(N_EXPERTS, dtype=I32)[None, :]).astype(I32)
    csum = jnp.cumsum(onehot, axis=0)
    rank = jnp.sum(onehot * (csum - 1), axis=1)
    counts = csum[-1]
    padded = (counts + tm - 1) // tm * tm
    pad_end = jnp.cumsum(padded)
    pad_start = pad_end - padded
    slot = (pad_start[flat_e] + rank).astype(I32)
    n_assign = flat_e.shape[0]
    n_tiles = n_assign // tm + N_EXPERTS
    slot_tok = jnp.zeros((n_tiles * tm,), I32).at[slot].set(jnp.arange(n_assign, dtype=I32) // TOP_K)
    tile_start = jnp.arange(n_tiles, dtype=I32) * tm
    tile_e = jnp.minimum(jnp.sum((pad_end[None, :] <= tile_start[:, None]).astype(I32), axis=1), N_EXPERTS - 1)
    n_valid = (pad_end[-1] // tm).astype(I32).reshape(1)
    return slot, slot_tok, tile_e, n_valid


def kernel(x_prompt, x_sample, cache_na_k, cache_na_v, cache_mla_ckv, cache_mla_krope, cache_gqa_k, cache_gqa_v,
           state_s5_re, state_s5_im, c, c_ctx, norm_g, w_ada, b_ada, w_in, na_qk_g, na_rpb, mla_q_lora_g,
           mla_kv_lora_g, mla_w_q_up, mla_w_kv_up, mla_nope_g, mla_rope_g, gqa_qk_g, s5_lambda_re, s5_lambda_im,
           s5_log_step, s5_b_re, s5_b_im, s5_c_re, s5_c_im, s5_d, s5_w_glu, w_branch, w_out, moe_w_router,
           moe_b_router, moe_w_gate, moe_b_gate, moe_w_up, moe_b_up, moe_w_down, moe_b_down):
    L, D = DEPTH, D_MODEL
    tm = 512
    ctx_blocks_1k = N_CTX // DEC_SEQ

    x = jnp.concatenate([x_prompt.reshape(N_CTX, D), x_sample.reshape(N_LAT, D)], axis=0).astype(F32)
    cond = jnp.zeros((MOD_ROWS, D), F32).at[0].set(c_ctx.astype(F32)).at[1:1 + DEC_BATCH].set(c.astype(F32))
    mod4 = _ada_modulation(cond, w_ada, b_ada).reshape(L, MOD_ROWS, 1, 6 * D)
    norm_g4 = norm_g.astype(F32).reshape(L, 2, 1, D)

    mla_tabs, mla_half = _rope_tables(MLA_ROPE)
    gqa_tabs, gqa_half = _rope_tables(HEAD_DIM)
    na_bias = _na_bias(na_rpb)
    lam_tiles, s5_wb, s5_wc = _s5_params(s5_lambda_re, s5_lambda_im, s5_log_step, s5_b_re, s5_b_im, s5_c_re, s5_c_im)
    h0_lat = _s5_state_tiles(state_s5_re.reshape(-1, 2, SSM_GROUPS, SSM_STATE),
                             state_s5_im.reshape(-1, 2, SSM_GROUPS, SSM_STATE)).reshape(DEC_BATCH, L, 128, LANE)
    h0_ctx = jnp.zeros((BATCH, 128, LANE), F32)
    s5_d3 = s5_d.astype(F32).reshape(L, 1, -1)

    wq = mla_w_q_up.reshape(L, MLA_Q_LORA, MLA_HEADS, MLA_NOPE + MLA_ROPE)
    wq_perm = jnp.concatenate([
        wq[..., :MLA_NOPE].reshape(L, MLA_Q_LORA, MLA_HEADS * MLA_NOPE),
        jnp.pad(wq[..., MLA_NOPE:], ((0, 0), (0, 0), (0, 0), (0, HEAD_DIM - MLA_ROPE))).reshape(L, MLA_Q_LORA, -1),
    ], axis=-1)
    wkv_perm = mla_w_kv_up.reshape(L, MLA_KV_LORA, MLA_HEADS, 2, HEAD_DIM).transpose(0, 1, 3, 2, 4).reshape(
        L, MLA_KV_LORA, 2 * MLA_HEADS * HEAD_DIM)
    rope_g_pad = jnp.pad(mla_rope_g.astype(F32), ((0, 0), (0, 0), (0, HEAD_DIM - MLA_ROPE)))
    cache_krope_pad = jnp.pad(cache_mla_krope, ((0, 0), (0, 0), (0, 0), (0, HEAD_DIM - MLA_ROPE)))
    cache_ckv_rows = cache_mla_ckv.reshape(DEC_BATCH * L * PAST_LEN, MLA_KV_LORA)
    c_na_k = cache_na_k.reshape(DEC_BATCH, L, PAST_LEN, NA_HEADS * HEAD_DIM)
    c_na_v = cache_na_v.reshape(DEC_BATCH, L, PAST_LEN, NA_HEADS * HEAD_DIM)
    c_gqa_k = cache_gqa_k.reshape(DEC_BATCH, L, PAST_LEN, GQA_KV_HEADS * HEAD_DIM)
    c_gqa_v = cache_gqa_v.reshape(DEC_BATCH, L, PAST_LEN, GQA_KV_HEADS * HEAD_DIM)

    wr = jnp.pad(moe_w_router.astype(F32), ((0, 0), (0, 0), (0, LANE - N_EXPERTS)))
    wr_hi = wr.astype(BF16)
    wr_lo = (wr - wr_hi.astype(F32)).astype(BF16)
    br = jnp.pad(moe_b_router.astype(F32), ((0, 0), (0, LANE - N_EXPERTS))).reshape(L, 1, LANE)
    b_gate4 = moe_b_gate.astype(F32).reshape(L, N_EXPERTS, 1, D_FF)
    b_up4 = moe_b_up.astype(F32).reshape(L, N_EXPERTS, 1, D_FF)
    b_down4 = moe_b_down.astype(F32).reshape(L, N_EXPERTS, 1, D)

    sm_scale = HEAD_DIM ** -0.5
    mla_scale = (MLA_NOPE + MLA_ROPE) ** -0.5
    g1 = lambda v: v.astype(F32).reshape(1, -1)

    def ctx_rows(width, cb):
        return (SEQ, width), lambda b, qi: (b, cb)

    def lat_rows(width, cb):
        return (DEC_SEQ, width), lambda b, qi: (ctx_blocks_1k + b, cb)

    def cache_rows(width, layer):
        return (None, None, PAST_LEN, width), lambda b, qi: (b, layer, 0, 0)

    def attend(q_parts, k_parts, v, ctx_src, *, k_heads, v_heads, bias=None, name):
        def src(rows):
            return ([(a, *rows(w, 0)) for a, w in k_parts], (v[0], *rows(v[1], v[2])))
        common = dict(heads=NA_HEADS, k_heads=k_heads, v_heads=v_heads)
        o_ctx = _attention(q_parts, [src(ctx_rows)], n_batch=BATCH, seq=SEQ, tq=SEQ, q_row0=0, name=name + "_ctx", **common)
        o_lat = _attention(q_parts, [src(lat_rows), ctx_src], n_batch=DEC_BATCH, seq=DEC_SEQ, tq=256, q_row0=N_CTX,
                           bias=bias, name=name + "_lat", **common)
        return jnp.concatenate([o_ctx, o_lat], axis=0)

    new = {k: [] for k in ("na_k", "na_v", "ckv", "krope", "gqa_k", "gqa_v", "s5")}
    for l in range(L):
        h = _norm1(x, norm_g4, mod4, l)
        win = functools.partial(_matmul, h, w_in, w_lead=(l,), tm=tm)
        z1 = win(col0=0, ncols=COL_BKR + 512, tn=512, out_dtype=F32, name="w_in_a")
        z2 = win(col0=COL_C, ncols=COL_GATE - COL_C, tn=512, out_dtype=F32, name="w_in_c")
        zg = _matmul(h, w_in, w_lead=(l,), tm=1024, col0=COL_GATE, ncols=N_BRANCHES * D, tn=512, out_dtype=BF16,
                     name="w_in_gate")

        qa = _headnorm(z1, g1(na_qk_g[l, 0]) * sm_scale, col_block=0, heads=NA_HEADS, name="na_q")
        ka, ka32 = _headnorm(z1, g1(na_qk_g[l, 1]), col_block=1, heads=NA_HEADS, f32_width=HEAD_DIM, name="na_k")
        o_a = attend([(qa, 0)], [(ka, 1024)], (z1, 1024, 2),
                     ([(c_na_k, *cache_rows(1024, l))], (c_na_v, *cache_rows(1024, l))),
                     k_heads=(NA_HEADS,), v_heads=NA_HEADS,
                     bias=(na_bias, (None, NA_HEADS, 256, DEC_SEQ), lambda b, qi, l=l: (l, 0, qi, 0)), name="na")
        new["na_k"].append(ka32[:N_CTX])
        new["na_v"].append(z1[:N_CTX, 2048:3072])

        ql = _headnorm(z1, g1(mla_q_lora_g[l]), col_block=3, heads=1, dh=MLA_Q_LORA, name="mla_ql")
        mq = _matmul(ql, wq_perm, w_lead=(l,), tm=tm, tn=512, out_dtype=F32, name="mla_q_up")
        mq_nope = _headnorm(mq, g1(mla_nope_g[l, 0]) * mla_scale, col_block=0, heads=MLA_HEADS, name="mla_q_nope")
        mq_rope = _headnorm(mq, g1(rope_g_pad[l, 0]) * mla_scale, col_block=1, heads=MLA_HEADS, dh_eff=MLA_ROPE,
                            rope=mla_tabs, rope_half=mla_half, name="mla_q_rope")
        ckv, ckv32 = _headnorm(z1, g1(mla_kv_lora_g[l]), col_block=COL_BKR // MLA_KV_LORA - 1, heads=1, dh=MLA_KV_LORA,
                               f32_width=MLA_KV_LORA, name="mla_ckv")
        kr, kr32 = _headnorm(z1, g1(rope_g_pad[l, 1]), col_block=COL_BKR // LANE, heads=1, dh_eff=MLA_ROPE, rope=mla_tabs,
                             rope_half=mla_half, f32_width=MLA_ROPE, name="mla_krope")
        kv = _matmul(ckv, wkv_perm, w_lead=(l,), tm=tm, tn=512, out_dtype=BF16, name="mla_kv_up")
        k_nope = _headnorm(kv, g1(mla_nope_g[l, 1]), col_block=0, heads=MLA_HEADS, name="mla_k_nope")
        kvc = _matmul(cache_ckv_rows, wkv_perm, w_lead=(l,), m_rows=DEC_BATCH * PAST_LEN, tm=PAST_LEN, tn=512,
                      a_index_map=lambda n, m, l=l: (m * L + l, 0), out_dtype=BF16, name="mla_kv_up_cache")
        kc_nope = _headnorm(kvc, g1(mla_nope_g[l, 1]), col_block=0, heads=MLA_HEADS, name="mla_kc_nope")
        ctx_src = ([(kc_nope, (PAST_LEN, 1024), lambda b, qi: (b, 0)), (cache_krope_pad, *cache_rows(HEAD_DIM, l))],
                   (kvc, (PAST_LEN, 1024), lambda b, qi: (b, 1)))
        o_b = attend([(mq_nope, 0), (mq_rope, 0)], [(k_nope, 1024), (kr, HEAD_DIM)], (kv, 1024, 1), ctx_src,
                     k_heads=(MLA_HEADS, 1), v_heads=MLA_HEADS, name="mla")
        new["ckv"].append(ckv32[:N_CTX])
        new["krope"].append(kr32[:N_CTX])

        gq = _headnorm(z2, g1(gqa_qk_g[l, 0]) * sm_scale, col_block=0, heads=GQA_Q_HEADS, rope=gqa_tabs,
                       rope_half=gqa_half, name="gqa_q")
        gk, gk32 = _headnorm(z2, g1(gqa_qk_g[l, 1]), col_block=4, heads=GQA_KV_HEADS, rope=gqa_tabs, rope_half=gqa_half,
                             f32_width=HEAD_DIM, name="gqa_k")
        o_c = attend([(gq, 0)], [(gk, 256)], (z2, 256, 5),
                     ([(c_gqa_k, *cache_rows(256, l))], (c_gqa_v, *cache_rows(256, l))),
                     k_heads=(GQA_KV_HEADS,), v_heads=GQA_KV_HEADS, name="gqa")
        new["gqa_k"].append(gk32[:N_CTX])
        new["gqa_v"].append(z2[:N_CTX, 1280:1536])

        u_cb = 1536 // 256
        bu3 = _matmul(z2, s5_wb, w_lead=(l,), ncols=2 * S5_GC * S5_CHUNK_COLS, tm=tm, tn=S5_CHUNK_COLS, out_dtype=F32,
                      k=256, a_index_map=lambda n, m: (m, u_cb + n % S5_GC), out_tiled=True, name="s5_bu")
        hf_c, hb_c, fin_c = _s5_scan(bu3, lam_tiles, h0_ctx, layer=l, n_batch=BATCH, seq=SEQ, row0=0)
        hf_l, hb_l, _ = _s5_scan(bu3, lam_tiles, h0_lat[:, l], layer=l, n_batch=DEC_BATCH, seq=DEC_SEQ, row0=N_CTX)
        new["s5"].append(fin_c)

        def s5_out(hf, hb, row_block0, n_rows):
            chunk = (tm, S5_CHUNK_COLS // LANE, LANE)
            return _multi_dot(
                [(hf, chunk, lambda n, m: (m, n, 0)), (hb, chunk, lambda n, m: (m, n, 0))],
                s5_wc, pl.BlockSpec((None, 2, None, S5_CHUNK_COLS, 256), lambda n, m: (l, 0, n, 0, 0)),
                m_rows=n_rows, ncols=BRANCH_WIDTH, tm=tm, tn=256, out_dtype=BF16,
                extra=[(z2, (tm, 256), lambda n, m: (row_block0 + m, u_cb + n)),
                       (s5_d3, (None, 1, 256), lambda n, m: (l, 0, n))],
                epilogue=lambda acc, u, dvec: jax.nn.gelu(acc + dvec * u), name="s5_out")

        yy = jnp.concatenate([s5_out(hf_c, hb_c, 0, N_CTX), s5_out(hf_l, hb_l, N_CTX // tm, N_LAT)], axis=0)
        o_d = _matmul(yy, s5_w_glu, w_lead=(l,), tm=tm, tn=512, out_dtype=BF16,
                      extra=[(yy, (tm, 512), lambda n, m: (m, n))],
                      epilogue=lambda acc, y: y.astype(F32) * jax.nn.sigmoid(acc), name="s5_glu")

        gate_blocks = D // 512
        merged = _multi_dot(
            [(o, (tm, BRANCH_WIDTH), lambda n, m: (m, 0)) for o in (o_a, o_b, o_c, o_d)],
            w_branch, pl.BlockSpec((None, N_BRANCHES, BRANCH_WIDTH, 512), lambda n, m: (l, 0, 0, n)),
            m_rows=N_TOK, ncols=D, tm=tm, tn=512, out_dtype=BF16,
            gates=[(zg, (tm, 512), lambda n, m, i=i: (m, i * gate_blocks + n)) for i in range(N_BRANCHES)],
            name="branch_merge")
        x = _matmul(merged, w_out, w_lead=(l,), tm=tm, tn=512, out_dtype=F32,
                    extra=[(x, (tm, 512), lambda n, m: (m, n)),
                           (mod4, (None, None, 1, 512), lambda n, m: (l, _mod_row(m * tm), 0, 2 * gate_blocks + n))],
                    epilogue=lambda acc, xv, g: xv + g * acc, name="w_out")

        hp, e_sel, p_sel = _norm2_router(x, norm_g4, mod4, wr_hi, wr_lo, br, l)
        slot, slot_tok, tile_e, n_valid = _moe_routing(e_sel)
        xs = _moe_gather(slot_tok, n_valid, hp)
        ys = _moe_experts(tile_e, n_valid, xs, moe_w_gate, b_gate4, moe_w_up, b_up4, moe_w_down, b_down4, l)
        x = _moe_combine(slot, ys, x, p_sel, mod4, l)

    y_prompt = x[:N_CTX].reshape(BATCH, SEQ, D).astype(x_prompt.dtype)
    y_sample = x[N_CTX:].reshape(DEC_BATCH, DEC_SEQ, D).astype(x_sample.dtype)
    stack = lambda key, shape: jnp.stack([t.reshape((BATCH, SEQ) + shape) for t in new[key]], axis=1)
    fin = jnp.stack(new["s5"], axis=1).reshape(BATCH * L, 2 * S5_HALF_ROWS, LANE)
    s5_re, s5_im = _s5_split_tiles(fin)
    s5_shape = (BATCH, L, 2, SSM_GROUPS, SSM_STATE)
    return (y_prompt, y_sample,
            stack("na_k", (NA_HEADS, HEAD_DIM)), stack("na_v", (NA_HEADS, HEAD_DIM)),
            stack("ckv", (MLA_KV_LORA,)), stack("krope", (MLA_ROPE,)),
            stack("gqa_k", (GQA_KV_HEADS, HEAD_DIM)), stack("gqa_v", (GQA_KV_HEADS, HEAD_DIM)),
            s5_re.reshape(s5_shape), s5_im.reshape(s5_shape))
```

```python
import functools
import math

import jax
import jax.numpy as jnp
import numpy as np
from jax import lax
from jax.experimental import pallas as pl
from jax.experimental.pallas import tpu as pltpu

F32 = jnp.float32
BF16 = jnp.bfloat16
U32 = jnp.uint32
I32 = jnp.int32

D_MODEL = 4096
BATCH = 32
SEQ = 256
DEPTH = 4
DEC_BATCH = 4
DEC_SEQ = 1024
PAST_LEN = 512
GRID_W = 64
HEAD_DIM = 128
N_BRANCHES = 4
BRANCH_WIDTH = D_MODEL // N_BRANCHES
ROPE_THETA = 10000.0
NEG_INF = -1e30
EPS = 1e-6
NA_HEADS = 8
NA_WIN_H = 8
NA_WIN_W = 16
MLA_HEADS = 8
MLA_NOPE = 128
MLA_ROPE = 64
MLA_Q_LORA = 1024
MLA_KV_LORA = 512
GQA_Q_HEADS = 8
GQA_KV_HEADS = 2
SSM_GROUP_CH = 16
SSM_GROUPS = 64
SSM_STATE = 64
N_EXPERTS = 32
TOP_K = 4
D_FF = 1024
SWIGLU_LIMIT = 7.0
SWIGLU_ALPHA = 1.702

N_CTX = BATCH * SEQ
N_LAT = DEC_BATCH * DEC_SEQ
N_TOK = N_CTX + N_LAT
MOD_ROWS = 8

COL_BKR = 3072 + MLA_Q_LORA + MLA_KV_LORA
COL_C = COL_BKR + MLA_ROPE
COL_GATE = COL_C + 1024 + 256 + 256 + 1024
IN_WIDTH = COL_GATE + N_BRANCHES * D_MODEL

LANE = 128
VMEM_LIMIT = 52 * 1024 * 1024
MOE_UP_VMEM_LIMIT = 57 * 1024 * 1024

S5_GC = 4
S5_CHUNK_COLS = 2048

MOE_TM = 512
MOE_TILES = N_TOK * TOP_K // MOE_TM + N_EXPERTS
MOE_SLOTS = MOE_TILES * MOE_TM
DMA_UNROLL = 8


def _cparams(n_axes, vmem=VMEM_LIMIT):
    return pltpu.CompilerParams(dimension_semantics=("arbitrary",) * n_axes, vmem_limit_bytes=vmem)


def _mod_row(row_start):
    return jnp.where(row_start < N_CTX, 0, 1 + (row_start - N_CTX) // DEC_SEQ)


def _cast_weight_tile(dst_ref, w_ref, w_next_ref, lane_shift, rows_per_step=256):
    k = w_ref.shape[0]
    rows = min(rows_per_step, k)

    def body(i, _):
        r = pl.multiple_of(i * rows, rows)
        w = w_ref[pl.ds(r, rows), :]
        if lane_shift:
            w = jnp.concatenate([w[:, lane_shift:], w_next_ref[pl.ds(r, rows), :lane_shift]], axis=1)
        dst_ref[pl.ds(r, rows), :] = w.astype(BF16)
        return 0

    lax.fori_loop(0, k // rows, body, 0)


def _store_lane_tiles(o_ref, val):
    for r in range(o_ref.shape[1]):
        o_ref[:, r, :] = val[:, r * LANE:(r + 1) * LANE].astype(o_ref.dtype)


def _load_lane_tiles(x_ref):
    return jnp.concatenate([x_ref[:, r, :] for r in range(x_ref.shape[1])], axis=1)


def _mm_kernel(*refs, lane_shift, n_extra, epilogue):
    a_ref, w_ref = refs[0], refs[1]
    pos = 2
    w_next_ref = None
    if lane_shift:
        w_next_ref = refs[pos]
        pos += 1
    extra = refs[pos:pos + n_extra]
    o_ref = refs[pos + n_extra]
    wb_ref = refs[pos + n_extra + 1]

    @pl.when(pl.program_id(1) == 0)
    def _():
        _cast_weight_tile(wb_ref, w_ref, w_next_ref, lane_shift)

    acc = jnp.dot(a_ref[...].astype(BF16), wb_ref[...], preferred_element_type=F32)
    res = epilogue(acc, *[e[...] for e in extra])
    if len(o_ref.shape) == 3:
        _store_lane_tiles(o_ref, res)
    else:
        o_ref[...] = res.astype(o_ref.dtype)


def _matmul(a, w, *, w_lead=(), col0=0, ncols=None, tm, tn, out_dtype, a_index_map=None, m_rows=None, k=None,
            extra=(), epilogue=None, out_tiled=False, name="matmul"):
    m_rows = m_rows or a.shape[0]
    a_index_map = a_index_map or (lambda n, m: (m, 0))
    k = k or w.shape[-2]
    ncols = ncols or w.shape[-1]
    lane_shift = col0 % LANE
    assert lane_shift in (0, 64) and m_rows % tm == 0 and ncols % tn == 0 and (col0 - lane_shift) % tn == 0
    nb0 = (col0 - lane_shift) // tn
    lead = tuple(w_lead)
    nlead = (None,) * len(lead)
    in_specs = [
        pl.BlockSpec((tm, k), a_index_map),
        pl.BlockSpec(nlead + (k, tn), lambda n, m: lead + (0, nb0 + n)),
    ]
    args = [a, w]
    if lane_shift:
        in_specs.append(pl.BlockSpec(nlead + (k, LANE), lambda n, m: lead + (0, (nb0 + n + 1) * (tn // LANE))))
        args.append(w)
    for arr, bshape, imap in extra:
        in_specs.append(pl.BlockSpec(bshape, imap))
        args.append(arr)
    epi = epilogue or (lambda acc: acc)
    if out_tiled:
        out_spec = pl.BlockSpec((tm, tn // LANE, LANE), lambda n, m: (m, n, 0))
        out_shape = jax.ShapeDtypeStruct((m_rows, ncols // LANE, LANE), out_dtype)
    else:
        out_spec = pl.BlockSpec((tm, tn), lambda n, m: (m, n))
        out_shape = jax.ShapeDtypeStruct((m_rows, ncols), out_dtype)
    return pl.pallas_call(
        functools.partial(_mm_kernel, lane_shift=lane_shift, n_extra=len(extra), epilogue=epi),
        grid=(ncols // tn, m_rows // tm),
        in_specs=in_specs,
        out_specs=out_spec,
        out_shape=out_shape,
        scratch_shapes=[pltpu.VMEM((k, tn), BF16)],
        compiler_params=_cparams(2),
        name=name,
    )(*args)


def _ada_kernel(c_ref, w_ref, b_ref, o_ref, wb_ref):
    _cast_weight_tile(wb_ref, w_ref, None, 0)
    c = c_ref[...]
    s = (c * jax.nn.sigmoid(c)).astype(BF16)
    o_ref[...] = jnp.dot(s, wb_ref[...], preferred_element_type=F32) + b_ref[...]


def _ada_modulation(cond, w_ada, b_ada, tn=512):
    depth, d, n = w_ada.shape
    rows = cond.shape[0]
    return pl.pallas_call(
        _ada_kernel,
        grid=(depth, n // tn),
        in_specs=[
            pl.BlockSpec((rows, d), lambda l, j: (0, 0)),
            pl.BlockSpec((None, d, tn), lambda l, j: (l, 0, j)),
            pl.BlockSpec((None, 1, tn), lambda l, j: (l, 0, j)),
        ],
        out_specs=pl.BlockSpec((None, rows, tn), lambda l, j: (l, 0, j)),
        out_shape=jax.ShapeDtypeStruct((depth, rows, n), F32),
        scratch_shapes=[pltpu.VMEM((d, tn), BF16)],
        compiler_params=_cparams(2),
        name="ada_modulation",
    )(cond, w_ada, b_ada.reshape(depth, 1, n))


def _pack_bf16_pair(lo, hi):
    lo_bits = lax.bitcast_convert_type(lo.astype(BF16).astype(F32), U32) >> 16
    hi_bits = lax.bitcast_convert_type(hi.astype(BF16).astype(F32), U32) & jnp.uint32(0xFFFF0000)
    return lo_bits | hi_bits


def _unpack_bf16_pair(word):
    lo = lax.bitcast_convert_type(word << 16, F32)
    hi = lax.bitcast_convert_type(word & jnp.uint32(0xFFFF0000), F32)
    return lo, hi


def _modulated_norm(x, g, scale, shift):
    y = x * lax.rsqrt(jnp.mean(x * x, axis=-1, keepdims=True) + EPS)
    return (y * g) * (1.0 + scale) + shift


def _norm_kernel(x_ref, g_ref, shift_ref, scale_ref, o_ref):
    o_ref[...] = _modulated_norm(x_ref[...], g_ref[...], scale_ref[...], shift_ref[...]).astype(o_ref.dtype)


def _split_hi_lo(v):
    hi = v.astype(BF16)
    return hi, (v - hi.astype(F32)).astype(BF16)


def _norm_router_kernel(x_ref, g_ref, shift_ref, scale_ref, wr_hi_ref, wr_lo_ref, br_ref,
                        hp_ref, e_ref, p_ref):
    h = _modulated_norm(x_ref[...], g_ref[...], scale_ref[...], shift_ref[...])
    half = h.shape[1] // 2
    hp_ref[...] = _pack_bf16_pair(h[:, :half], h[:, half:])
    h_hi, h_lo = _split_hi_lo(h)
    logits = (jnp.dot(h_hi, wr_hi_ref[...], preferred_element_type=F32)
              + jnp.dot(h_hi, wr_lo_ref[...], preferred_element_type=F32)
              + jnp.dot(h_lo, wr_hi_ref[...], preferred_element_type=F32)) + br_ref[...]
    lane = lax.broadcasted_iota(I32, logits.shape, 1)
    logits = jnp.where(lane < N_EXPERTS, logits, -jnp.inf)
    e_out = jnp.zeros(logits.shape, I32)
    p_out = jnp.zeros(logits.shape, F32)
    top = None
    denom = None
    for kk in range(TOP_K):
        m = jnp.max(logits, axis=-1, keepdims=True)
        idx = jnp.min(jnp.where(logits == m, lane, LANE), axis=-1, keepdims=True)
        if kk == 0:
            top = m
        p = jnp.exp(m - top)
        denom = p if kk == 0 else denom + p
        e_out = jnp.where(lane == kk, idx, e_out)
        p_out = jnp.where(lane == kk, p, p_out)
        logits = jnp.where(lane == idx, -jnp.inf, logits)
    e_ref[...] = e_out
    p_ref[...] = p_out / denom


def _norm_specs(tm, d, layer, which):
    return [
        pl.BlockSpec((tm, d), lambda i: (i, 0)),
        pl.BlockSpec((None, None, 1, d), lambda i: (layer, which, 0, 0)),
        pl.BlockSpec((None, None, 1, d), lambda i: (layer, _mod_row(i * tm), 0, 3 * which)),
        pl.BlockSpec((None, None, 1, d), lambda i: (layer, _mod_row(i * tm), 0, 3 * which + 1)),
    ]


def _norm1(x, norm_g4, mod4, layer, tm=256):
    n, d = x.shape
    return pl.pallas_call(
        _norm_kernel,
        grid=(n // tm,),
        in_specs=_norm_specs(tm, d, layer, 0),
        out_specs=pl.BlockSpec((tm, d), lambda i: (i, 0)),
        out_shape=jax.ShapeDtypeStruct((n, d), BF16),
        compiler_params=_cparams(1),
        name="norm1",
    )(x, norm_g4, mod4, mod4)


def _norm2_router(x, norm_g4, mod4, wr_hi, wr_lo, br, layer, tm=256):
    n, d = x.shape
    const = lambda i: (layer, 0, 0)
    return pl.pallas_call(
        _norm_router_kernel,
        grid=(n // tm,),
        in_specs=_norm_specs(tm, d, layer, 1) + [
            pl.BlockSpec((None, d, LANE), const),
            pl.BlockSpec((None, d, LANE), const),
            pl.BlockSpec((None, 1, LANE), const),
        ],
        out_specs=[
            pl.BlockSpec((tm, d // 2), lambda i: (i, 0)),
            pl.BlockSpec((tm, LANE), lambda i: (i, 0)),
            pl.BlockSpec((tm, LANE), lambda i: (i, 0)),
        ],
        out_shape=[
            jax.ShapeDtypeStruct((n, d // 2), U32),
            jax.ShapeDtypeStruct((n, LANE), I32),
            jax.ShapeDtypeStruct((n, LANE), F32),
        ],
        compiler_params=_cparams(1),
        name="norm2_router",
    )(x, norm_g4, mod4, mod4, wr_hi, wr_lo, br)


def _headnorm_kernel(*refs, heads, dh, dh_eff, rope_half, f32_width):
    x_ref, g_ref = refs[0], refs[1]
    pos = 2
    tabs = None
    if rope_half:
        tabs = [r[...] for r in refs[pos:pos + 3]]
        pos += 3
    ob_ref = refs[pos]
    of_ref = refs[pos + 1] if f32_width else None
    g = g_ref[...]
    for h in range(heads):
        x = x_ref[:, h * dh:(h + 1) * dh].astype(F32)
        if dh_eff < dh:
            lane = lax.broadcasted_iota(I32, x.shape, 1)
            x = jnp.where(lane < dh_eff, x, 0.0)
        y = x * lax.rsqrt(jnp.sum(x * x, axis=-1, keepdims=True) * (1.0 / dh_eff) + EPS) * g
        if rope_half:
            c, sa, sb = tabs
            y = y * c + pltpu.roll(y, rope_half, 1) * sa + pltpu.roll(y, dh - rope_half, 1) * sb
        ob_ref[:, h * dh:(h + 1) * dh] = y.astype(ob_ref.dtype)
        if of_ref is not None:
            of_ref[:, h * f32_width:(h + 1) * f32_width] = y[:, :f32_width]


def _headnorm(x, gain, *, col_block, heads, dh=HEAD_DIM, dh_eff=None, rope=None, rope_half=0,
              f32_width=0, tm=256, name="headnorm"):
    n = x.shape[0]
    w = heads * dh
    dh_eff = dh_eff or dh
    in_specs = [pl.BlockSpec((tm, w), lambda i: (i, col_block)), pl.BlockSpec((1, dh), lambda i: (0, 0))]
    args = [x, gain]
    if rope is not None:
        in_specs += [pl.BlockSpec((tm, dh), lambda i: (i, 0))] * 3
        args += list(rope)
    out_specs = [pl.BlockSpec((tm, w), lambda i: (i, 0))]
    out_shape = [jax.ShapeDtypeStruct((n, w), BF16)]
    if f32_width:
        out_specs.append(pl.BlockSpec((tm, heads * f32_width), lambda i: (i, 0)))
        out_shape.append(jax.ShapeDtypeStruct((n, heads * f32_width), F32))
    out = pl.pallas_call(
        functools.partial(_headnorm_kernel, heads=heads, dh=dh, dh_eff=dh_eff,
                          rope_half=rope_half if rope is not None else 0, f32_width=f32_width),
        grid=(n // tm,),
        in_specs=in_specs,
        out_specs=out_specs,
        out_shape=out_shape,
        compiler_params=_cparams(1),
        name=name,
    )(*args)
    return out if f32_width else out[0]


def _attn_kernel(*refs, heads, n_parts, k_heads, v_heads, n_src, has_bias):
    d = HEAD_DIM
    q_refs = refs[:n_parts]
    pos = n_parts
    srcs = []
    for _ in range(n_src):
        srcs.append((refs[pos:pos + n_parts], refs[pos + n_parts]))
        pos += n_parts + 1
    bias_ref = refs[pos] if has_bias else None
    o_ref = refs[-1]
    contract_last = (((1,), (1,)), ((), ()))
    for h in range(heads):
        scores = []
        for si, (k_refs, _) in enumerate(srcs):
            s = None
            for p in range(n_parts):
                kh = h // (heads // k_heads[p])
                q = q_refs[p][:, h * d:(h + 1) * d]
                k = k_refs[p][:, kh * d:(kh + 1) * d].astype(BF16)
                part = lax.dot_general(q, k, contract_last, preferred_element_type=F32)
                s = part if s is None else s + part
            if si == 0 and has_bias:
                s = s + bias_ref[h]
            scores.append(s)
        m = functools.reduce(jnp.maximum, [jnp.max(s, axis=-1, keepdims=True) for s in scores])
        probs = [jnp.exp(s - m) for s in scores]
        denom = functools.reduce(lambda a, b: a + b, [jnp.sum(p, axis=-1, keepdims=True) for p in probs])
        vh = h // (heads // v_heads)
        o = None
        for p, (_, v_ref) in zip(probs, srcs):
            v = v_ref[:, vh * d:(vh + 1) * d].astype(BF16)
            pv = jnp.dot(p.astype(BF16), v, preferred_element_type=F32)
            o = pv if o is None else o + pv
        o_ref[:, h * d:(h + 1) * d] = (o / denom).astype(o_ref.dtype)


def _attention(q_parts, sources, *, n_batch, seq, tq, q_row0, heads, k_heads, v_heads, bias=None, name="attention"):
    w = heads * HEAD_DIM
    nq = seq // tq
    qb0 = q_row0 // tq
    in_specs, args = [], []
    for arr, cb in q_parts:
        in_specs.append(pl.BlockSpec((tq, w), lambda b, qi, cb=cb: (qb0 + b * nq + qi, cb)))
        args.append(arr)
    for k_parts, v in sources:
        for arr, bshape, imap in list(k_parts) + [v]:
            in_specs.append(pl.BlockSpec(bshape, imap))
            args.append(arr)
    if bias is not None:
        in_specs.append(pl.BlockSpec(bias[1], bias[2]))
        args.append(bias[0])
    return pl.pallas_call(
        functools.partial(_attn_kernel, heads=heads, n_parts=len(q_parts), k_heads=k_heads, v_heads=v_heads,
                          n_src=len(sources), has_bias=bias is not None),
        grid=(n_batch, nq),
        in_specs=in_specs,
        out_specs=pl.BlockSpec((tq, w), lambda b, qi: (b * nq + qi, 0)),
        out_shape=jax.ShapeDtypeStruct((n_batch * seq, w), BF16),
        compiler_params=_cparams(2),
        name=name,
    )(*args)


S5_HALF_ROWS = 64


def _s5_step(h, bu, lam):
    h4 = h.reshape(S5_GC, 2, 8, LANE)
    b4 = bu.reshape(S5_GC, 2, 8, LANE)
    a4 = lam.reshape(S5_GC, 2, 8, LANE)
    hr, hi = h4[:, 0], h4[:, 1]
    ar, ai = a4[:, 0], a4[:, 1]
    nr = ar * hr - ai * hi + b4[:, 0]
    ni = ar * hi + ai * hr + b4[:, 1]
    return jnp.stack([nr, ni], axis=1).reshape(S5_HALF_ROWS, LANE)


def _s5_scan_kernel(buf_ref, bub_ref, lam_ref, h0_ref, hf_ref, hb_ref, fin_ref, state_ref, *, tc):
    j = pl.program_id(1)

    @pl.when(j == 0)
    def _():
        state_ref[...] = h0_ref[...]

    lam_f = lam_ref[:S5_HALF_ROWS, :]
    lam_b = lam_ref[S5_HALF_ROWS:, :]

    def body(i, carry):
        hf, hb = carry
        hf = _s5_step(hf, buf_ref[i], lam_f)
        hb = _s5_step(hb, bub_ref[tc - 1 - i], lam_b)
        hf_ref[i] = hf.astype(hf_ref.dtype)
        hb_ref[tc - 1 - i] = hb.astype(hb_ref.dtype)
        return hf, hb

    hf, hb = lax.fori_loop(0, tc, body, (state_ref[:S5_HALF_ROWS, :], state_ref[S5_HALF_ROWS:, :]), unroll=4)
    state_ref[:S5_HALF_ROWS, :] = hf
    state_ref[S5_HALF_ROWS:, :] = hb

    @pl.when(j == pl.num_programs(1) - 1)
    def _():
        fin_ref[...] = state_ref[...]


def _s5_scan(bu3, lam_tiles, h0, *, layer, n_batch, seq, row0, tc=64):
    nt = seq // tc
    rb0 = row0 // tc
    n = n_batch * seq
    return pl.pallas_call(
        functools.partial(_s5_scan_kernel, tc=tc),
        grid=(n_batch, nt),
        in_specs=[
            pl.BlockSpec((tc, S5_HALF_ROWS, LANE), lambda b, j: (rb0 + b * nt + j, 0, 0)),
            pl.BlockSpec((tc, S5_HALF_ROWS, LANE), lambda b, j: (rb0 + b * nt + (nt - 1 - j), 1, 0)),
            pl.BlockSpec((None, 2 * S5_HALF_ROWS, LANE), lambda b, j: (layer, 0, 0)),
            pl.BlockSpec((None, 2 * S5_HALF_ROWS, LANE), lambda b, j: (b, 0, 0)),
        ],
        out_specs=[
            pl.BlockSpec((tc, S5_HALF_ROWS, LANE), lambda b, j: (b * nt + j, 0, 0)),
            pl.BlockSpec((tc, S5_HALF_ROWS, LANE), lambda b, j: (b * nt + (nt - 1 - j), 0, 0)),
            pl.BlockSpec((None, 2 * S5_HALF_ROWS, LANE), lambda b, j: (b, 0, 0)),
        ],
        out_shape=[
            jax.ShapeDtypeStruct((n, S5_HALF_ROWS, LANE), BF16),
            jax.ShapeDtypeStruct((n, S5_HALF_ROWS, LANE), BF16),
            jax.ShapeDtypeStruct((n_batch, 2 * S5_HALF_ROWS, LANE), F32),
        ],
        scratch_shapes=[pltpu.VMEM((2 * S5_HALF_ROWS, LANE), F32)],
        compiler_params=_cparams(2),
        name="s5_scan",
    )(bu3, bu3, lam_tiles, h0)


def _multi_dot_kernel(*refs, n_terms, gated, n_extra, epilogue):
    a_refs = refs[:n_terms]
    w_ref = refs[n_terms]
    pos = n_terms + 1
    g_refs = refs[pos:pos + n_terms] if gated else None
    pos += n_terms if gated else 0
    extra = refs[pos:pos + n_extra]
    o_ref = refs[pos + n_extra]
    wb_ref = refs[pos + n_extra + 1]

    @pl.when(pl.program_id(1) == 0)
    def _():
        for t in range(n_terms):
            _cast_weight_tile(wb_ref.at[t], w_ref.at[t], None, 0)

    acc = None
    for t in range(n_terms):
        a = _load_lane_tiles(a_refs[t]) if len(a_refs[t].shape) == 3 else a_refs[t][...]
        part = jnp.dot(a.astype(BF16), wb_ref[t], preferred_element_type=F32)
        if gated:
            part = part * jax.nn.sigmoid(g_refs[t][...].astype(F32))
        acc = part if acc is None else acc + part
    o_ref[...] = epilogue(acc, *[e[...] for e in extra]).astype(o_ref.dtype)


def _multi_dot(a_terms, w, w_spec, *, m_rows, ncols, tm, tn, out_dtype, gates=None, extra=(), epilogue=None,
               name="multi_dot"):
    n_terms = len(a_terms)
    kdim = math.prod(a_terms[0][1][1:])
    in_specs = [pl.BlockSpec(bs, im) for _, bs, im in a_terms] + [w_spec]
    args = [a for a, _, _ in a_terms] + [w]
    if gates is not None:
        in_specs += [pl.BlockSpec(bs, im) for _, bs, im in gates]
        args += [g for g, _, _ in gates]
    for arr, bshape, imap in extra:
        in_specs.append(pl.BlockSpec(bshape, imap))
        args.append(arr)
    epi = epilogue or (lambda acc: acc)
    return pl.pallas_call(
        functools.partial(_multi_dot_kernel, n_terms=n_terms, gated=gates is not None, n_extra=len(extra), epilogue=epi),
        grid=(ncols // tn, m_rows // tm),
        in_specs=in_specs,
        out_specs=pl.BlockSpec((tm, tn), lambda n, m: (m, n)),
        out_shape=jax.ShapeDtypeStruct((m_rows, ncols), out_dtype),
        scratch_shapes=[pltpu.VMEM((n_terms, kdim, tn), BF16)],
        compiler_params=_cparams(2),
        name=name,
    )(*args)


def _row_gather(idx_of, src_hbm, dst_of, sem, n_rows):
    def copy(r):
        return pltpu.make_async_copy(src_hbm.at[pl.ds(idx_of(r), 1)], dst_of(r), sem)

    def start():
        def body(g, c):
            for j in range(DMA_UNROLL):
                copy(g * DMA_UNROLL + j).start(priority=j % 2)
            return c
        lax.fori_loop(0, n_rows // DMA_UNROLL, body, 0)

    def wait():
        def body(g, c):
            for j in range(DMA_UNROLL):
                copy(g * DMA_UNROLL + j).wait()
            return c
        lax.fori_loop(0, n_rows // DMA_UNROLL, body, 0)

    return start, wait


def _moe_gather_kernel(tok_ref, nv_ref, hp_hbm, o_ref, buf, sem, *, tm):
    t = pl.program_id(0)
    nv = nv_ref[0]

    def tile_ops(tile, slot):
        return _row_gather(lambda r: tok_ref[tile * tm + r], hp_hbm,
                           lambda r: buf.at[slot, pl.ds(r, 1)], sem.at[slot], tm)

    @pl.when((t == 0) & (nv > 0))
    def _():
        tile_ops(0, 0)[0]()

    @pl.when(t + 1 < nv)
    def _():
        tile_ops(t + 1, (t + 1) % 2)[0]()

    @pl.when(t < nv)
    def _():
        slot = t % 2
        tile_ops(t, slot)[1]()
        half = buf.shape[2]
        chunk = 512
        for c0 in range(0, half, chunk):
            lo, hi = _unpack_bf16_pair(buf[slot, :, c0:c0 + chunk])
            o_ref[:, c0:c0 + chunk] = lo.astype(BF16)
            o_ref[:, half + c0:half + c0 + chunk] = hi.astype(BF16)

    @pl.when(t >= nv)
    def _():
        o_ref[...] = jnp.zeros_like(o_ref)


def _moe_gather(slot_tok, n_valid, hp, tm=MOE_TM):
    n_slots = slot_tok.shape[0]
    half = hp.shape[1]
    return pl.pallas_call(
        functools.partial(_moe_gather_kernel, tm=tm),
        grid_spec=pltpu.PrefetchScalarGridSpec(
            num_scalar_prefetch=2,
            grid=(n_slots // tm,),
            in_specs=[pl.BlockSpec(memory_space=pl.ANY)],
            out_specs=pl.BlockSpec((tm, 2 * half), lambda t, tok, nv: (t, 0)),
            scratch_shapes=[pltpu.VMEM((2, tm, half), U32), pltpu.SemaphoreType.DMA((2,))],
        ),
        out_shape=jax.ShapeDtypeStruct((n_slots, 2 * half), BF16),
        compiler_params=_cparams(1),
        name="moe_gather",
    )(slot_tok, n_valid, hp)


def _moe_tile_flags(te_ref, nv_ref, t):
    valid = t < nv_ref[0]
    new_expert = (t == 0) | (te_ref[t] != te_ref[jnp.maximum(t - 1, 0)])
    return valid, valid & new_expert


def _moe_up_kernel(te_ref, nv_ref, xs_ref, wg_ref, wu_ref, bg_ref, bu_ref, o_ref, wb_ref):
    valid, recast = _moe_tile_flags(te_ref, nv_ref, pl.program_id(1))

    @pl.when(recast)
    def _():
        _cast_weight_tile(wb_ref.at[0], wg_ref, None, 0)
        _cast_weight_tile(wb_ref.at[1], wu_ref, None, 0)

    @pl.when(valid)
    def _():
        x = xs_ref[...]
        gate = jnp.minimum(jnp.dot(x, wb_ref[0], preferred_element_type=F32) + bg_ref[...], SWIGLU_LIMIT)
        up = jnp.clip(jnp.dot(x, wb_ref[1], preferred_element_type=F32) + bu_ref[...], -SWIGLU_LIMIT, SWIGLU_LIMIT)
        o_ref[...] = (gate * jax.nn.sigmoid(SWIGLU_ALPHA * gate) * (up + 1.0)).astype(o_ref.dtype)

    @pl.when(jnp.logical_not(valid))
    def _():
        o_ref[...] = jnp.zeros_like(o_ref)


def _moe_down_kernel(te_ref, nv_ref, act_ref, wd_ref, bd_ref, o_ref, wb_ref):
    valid, recast = _moe_tile_flags(te_ref, nv_ref, pl.program_id(1))

    @pl.when(recast)
    def _():
        _cast_weight_tile(wb_ref, wd_ref, None, 0)

    @pl.when(valid)
    def _():
        y = jnp.dot(act_ref[...], wb_ref[...], preferred_element_type=F32) + bd_ref[...]
        half = y.shape[1] // 2
        o_ref[...] = _pack_bf16_pair(y[:, :half], y[:, half:])

    @pl.when(jnp.logical_not(valid))
    def _():
        o_ref[...] = jnp.zeros_like(o_ref)


def _moe_experts(tile_e, n_valid, xs, w_gate, b_gate, w_up, b_up, w_down, b_down, layer, tm=MOE_TM, tf=512, tn=2048):
    n_slots, d = xs.shape
    d_ff = w_gate.shape[-1]
    n_tiles = n_slots // tm
    row = lambda t, nv: jnp.minimum(t, jnp.maximum(nv[0] - 1, 0))
    act = pl.pallas_call(
        _moe_up_kernel,
        grid_spec=pltpu.PrefetchScalarGridSpec(
            num_scalar_prefetch=2,
            grid=(d_ff // tf, n_tiles),
            in_specs=[
                pl.BlockSpec((tm, d), lambda f, t, te, nv: (row(t, nv), 0)),
                pl.BlockSpec((None, None, d, tf), lambda f, t, te, nv: (layer, te[t], 0, f)),
                pl.BlockSpec((None, None, d, tf), lambda f, t, te, nv: (layer, te[t], 0, f)),
                pl.BlockSpec((None, None, 1, tf), lambda f, t, te, nv: (layer, te[t], 0, f)),
                pl.BlockSpec((None, None, 1, tf), lambda f, t, te, nv: (layer, te[t], 0, f)),
            ],
            out_specs=pl.BlockSpec((tm, tf), lambda f, t, te, nv: (t, f)),
            scratch_shapes=[pltpu.VMEM((2, d, tf), BF16)],
        ),
        out_shape=jax.ShapeDtypeStruct((n_slots, d_ff), BF16),
        compiler_params=_cparams(2, MOE_UP_VMEM_LIMIT),
        name="moe_up",
    )(tile_e, n_valid, xs, w_gate, w_up, b_gate, b_up)
    return pl.pallas_call(
        _moe_down_kernel,
        grid_spec=pltpu.PrefetchScalarGridSpec(
            num_scalar_prefetch=2,
            grid=(d // tn, n_tiles),
            in_specs=[
                pl.BlockSpec((tm, d_ff), lambda n, t, te, nv: (row(t, nv), 0)),
                pl.BlockSpec((None, None, d_ff, tn), lambda n, t, te, nv: (layer, te[t], 0, n)),
                pl.BlockSpec((None, None, 1, tn), lambda n, t, te, nv: (layer, te[t], 0, n)),
            ],
            out_specs=pl.BlockSpec((tm, tn // 2), lambda n, t, te, nv: (t, n)),
            scratch_shapes=[pltpu.VMEM((d_ff, tn), BF16)],
        ),
        out_shape=jax.ShapeDtypeStruct((n_slots, d // 2), U32),
        compiler_params=_cparams(2),
        name="moe_down",
    )(tile_e, n_valid, act, w_down, b_down)


def _moe_combine_kernel(slot_ref, ys_hbm, x_ref, p_ref, gate_ref, o_ref, buf, sem, *, tt, tn):
    i = pl.program_id(0)
    n_steps = pl.num_programs(0)
    rows = tt * TOP_K

    def tile_ops(tile, slot):
        return _row_gather(lambda r: slot_ref[tile * rows + r], ys_hbm,
                           lambda r: buf.at[slot, r & (TOP_K - 1), pl.ds(lax.shift_right_logical(r, TOP_K.bit_length() - 1), 1)],
                           sem.at[slot], rows)

    @pl.when(i == 0)
    def _():
        tile_ops(0, 0)[0]()

    @pl.when(i + 1 < n_steps)
    def _():
        tile_ops(i + 1, (i + 1) % 2)[0]()

    slot = i % 2
    tile_ops(i, slot)[1]()
    p = p_ref[...]
    pk = [p[:, kk:kk + 1] for kk in range(TOP_K)]
    hw = tn // 2
    chunk = 512
    for w0 in range(0, x_ref.shape[1] // 2, chunk):
        lo = hi = None
        for kk in range(TOP_K):
            w_lo, w_hi = _unpack_bf16_pair(buf[slot, kk, :, w0:w0 + chunk])
            lo = w_lo * pk[kk] if lo is None else lo + w_lo * pk[kk]
            hi = w_hi * pk[kk] if hi is None else hi + w_hi * pk[kk]
        c_lo = (w0 // hw) * tn + w0 % hw
        c_hi = c_lo + hw
        o_ref[:, c_lo:c_lo + chunk] = x_ref[:, c_lo:c_lo + chunk] + gate_ref[:, c_lo:c_lo + chunk] * lo
        o_ref[:, c_hi:c_hi + chunk] = x_ref[:, c_hi:c_hi + chunk] + gate_ref[:, c_hi:c_hi + chunk] * hi


def _moe_combine(slots, ys, x, p_sel, mod4, layer, tt=128, tn=2048):
    n, d = x.shape
    return pl.pallas_call(
        functools.partial(_moe_combine_kernel, tt=tt, tn=tn),
        grid_spec=pltpu.PrefetchScalarGridSpec(
            num_scalar_prefetch=1,
            grid=(n // tt,),
            in_specs=[
                pl.BlockSpec(memory_space=pl.ANY),
                pl.BlockSpec((tt, d), lambda i, s: (i, 0)),
                pl.BlockSpec((tt, LANE), lambda i, s: (i, 0)),
                pl.BlockSpec((None, None, 1, d), lambda i, s: (layer, _mod_row(i * tt), 0, 5)),
            ],
            out_specs=pl.BlockSpec((tt, d), lambda i, s: (i, 0)),
            scratch_shapes=[pltpu.VMEM((2, TOP_K, tt, d // 2), U32), pltpu.SemaphoreType.DMA((2,))],
        ),
        out_shape=jax.ShapeDtypeStruct((n, d), F32),
        compiler_params=_cparams(1),
        name="moe_combine",
    )(slots, ys, x, p_sel, mod4)


def _rope_tables(rot_dim):
    t = jnp.arange(DEC_SEQ)
    n_freq = rot_dim // 4
    freqs = ROPE_THETA ** (-jnp.arange(n_freq, dtype=F32) / n_freq)
    ang = jnp.concatenate([(t // GRID_W).astype(F32)[:, None] * freqs,
                           (t % GRID_W).astype(F32)[:, None] * freqs], axis=-1)
    cos, sin = jnp.cos(ang), jnp.sin(ang)
    half = rot_dim // 2
    pad = HEAD_DIM - rot_dim
    zeros = jnp.zeros_like(sin)

    def full(lat, ctx_row):
        lat = jnp.pad(lat, ((0, 0), (0, pad)))
        ctx = jnp.broadcast_to(jnp.pad(ctx_row, (0, pad))[None], (N_CTX, HEAD_DIM))
        return jnp.concatenate([ctx, jnp.tile(lat, (DEC_BATCH, 1))], axis=0)

    one = jnp.ones((rot_dim,), F32)
    zero = jnp.zeros((rot_dim,), F32)
    c = full(jnp.concatenate([cos, cos], axis=-1), one)
    sa = full(jnp.concatenate([zeros, sin], axis=-1), zero)
    sb = full(jnp.concatenate([-sin, zeros], axis=-1), zero)
    return (c, sa, sb), half


def _na_bias(na_rpb):
    rows = DEC_SEQ // GRID_W
    wh = min(NA_WIN_H, rows)
    r = np.arange(rows)
    row_lo = np.clip(r - wh // 2, 0, rows - wh)
    row_ok = (r[None, :] >= row_lo[:, None]) & (r[None, :] < row_lo[:, None] + wh)
    d_row = np.clip(r[None, :] - r[:, None] + (NA_WIN_H - 1), 0, 2 * NA_WIN_H - 2)
    col = np.arange(GRID_W)
    col_lo = np.clip(col - NA_WIN_W // 2, 0, GRID_W - NA_WIN_W)
    col_ok = (col[None, :] >= col_lo[:, None]) & (col[None, :] < col_lo[:, None] + NA_WIN_W)
    d_col = np.clip(col[None, :] - col[:, None], -(NA_WIN_W - 1), NA_WIN_W - 1) + (NA_WIN_W - 1)
    sel_row = (d_row[:, :, None] == np.arange(2 * NA_WIN_H - 1)).astype(np.float32)
    sel_col = (d_col[:, :, None] == np.arange(2 * NA_WIN_W - 1)).astype(np.float32)
    rpb = na_rpb.astype(F32)
    t1 = jnp.einsum('lhad,rsa->lhrsd', rpb, sel_row, precision=lax.Precision.HIGHEST)
    t2 = jnp.einsum('lhrsd,qkd->lhrqsk', t1, sel_col, precision=lax.Precision.HIGHEST)
    ok = row_ok[:, None, :, None] & col_ok[None, :, None, :]
    bias = jnp.where(ok[None, None], t2, NEG_INF)
    return bias.reshape(DEPTH, NA_HEADS, DEC_SEQ, DEC_SEQ)


def _s5_params(lam_re, lam_im, log_step, b_re, b_im, c_re, c_im):
    step = jnp.exp(log_step.astype(F32))[..., None]
    lr, li = lam_re.astype(F32), lam_im.astype(F32)
    mag = jnp.exp(lr * step)
    ar, ai = mag * jnp.cos(li * step), mag * jnp.sin(li * step)
    den = lr * lr + li * li
    kr = ((ar - 1.0) * lr + ai * li) / den
    ki = (ai * lr - (ar - 1.0) * li) / den
    bbr = kr[..., None] * b_re - ki[..., None] * b_im
    bbi = kr[..., None] * b_im + ki[..., None] * b_re
    gl = SSM_GROUPS // S5_GC
    eye = jnp.eye(gl, dtype=F32)

    def tiles(re, im):
        v = jnp.stack([re, im], axis=3)
        v = v.reshape(DEPTH, 2, S5_GC, gl, 2, SSM_STATE).transpose(0, 1, 2, 4, 3, 5)
        return v.reshape(DEPTH, 2 * S5_HALF_ROWS, LANE)

    lam_tiles = tiles(ar, ai)
    bb = jnp.stack([bbr, bbi], axis=-1)
    bb = bb.reshape(DEPTH, 2, S5_GC, gl, SSM_STATE, SSM_GROUP_CH, 2)
    wb = bb.transpose(0, 1, 2, 3, 5, 6, 4)[:, :, :, :, :, :, None, :] * eye[None, None, None, :, None, None, :, None]
    wb = wb.reshape(DEPTH, 2 * S5_GC, gl * SSM_GROUP_CH, S5_CHUNK_COLS)
    wb = wb.transpose(0, 2, 1, 3).reshape(DEPTH, gl * SSM_GROUP_CH, 2 * S5_GC * S5_CHUNK_COLS)
    cc = jnp.stack([c_re.astype(F32), -c_im.astype(F32)], axis=2)
    cc = cc.reshape(DEPTH, 2, 2, S5_GC, gl, SSM_GROUP_CH, SSM_STATE)
    wc = cc.transpose(0, 1, 3, 2, 4, 6, 5)[:, :, :, :, :, :, None, :] * eye[None, None, None, None, :, None, :, None]
    wc = wc.reshape(DEPTH, 2, S5_GC, S5_CHUNK_COLS, gl * SSM_GROUP_CH)
    return lam_tiles, wb.astype(BF16), wc.astype(BF16)


def _s5_state_tiles(re, im):
    gl = SSM_GROUPS // S5_GC
    v = jnp.stack([re.astype(F32), im.astype(F32)], axis=3)
    v = v.reshape(-1, 2, S5_GC, gl, 2, SSM_STATE).transpose(0, 1, 2, 4, 3, 5)
    return v.reshape(-1, 2 * S5_HALF_ROWS, LANE)


def _s5_split_tiles(tiles):
    gl = SSM_GROUPS // S5_GC
    v = tiles.reshape(-1, 2, S5_GC, 2, gl, SSM_STATE).transpose(0, 1, 3, 2, 4, 5)
    v = v.reshape(-1, 2, 2, SSM_GROUPS, SSM_STATE)
    return v[:, :, 0], v[:, :, 1]


def _moe_routing(e_sel, tm=MOE_TM):
    flat_e = e_sel[:, :TOP_K].reshape(-1)
    onehot = (flat_e[:, None] == jnp.arange(N_EXPERTS, dtype=I32)[None, :]).astype(I32)
    csum = jnp.cumsum(onehot, axis=0)
    rank = jnp.sum(onehot * (csum - 1), axis=1)
    counts = csum[-1]
    padded = (counts + tm - 1) // tm * tm
    pad_end = jnp.cumsum(padded)
    pad_start = pad_end - padded
    slot = (pad_start[flat_e] + rank).astype(I32)
    n_assign = flat_e.shape[0]
    n_tiles = n_assign // tm + N_EXPERTS
    slot_tok = jnp.zeros((n_tiles * tm,), I32).at[slot].set(jnp.arange(n_assign, dtype=I32) // TOP_K)
    tile_start = jnp.arange(n_tiles, dtype=I32) * tm
    tile_e = jnp.minimum(jnp.sum((pad_end[None, :] <= tile_start[:, None]).astype(I32), axis=1), N_EXPERTS - 1)
    n_valid = (pad_end[-1] // tm).astype(I32).reshape(1)
    return slot, slot_tok, tile_e, n_valid


def kernel(x_prompt, x_sample, cache_na_k, cache_na_v, cache_mla_ckv, cache_mla_krope, cache_gqa_k, cache_gqa_v,
           state_s5_re, state_s5_im, c, c_ctx, norm_g, w_ada, b_ada, w_in, na_qk_g, na_rpb, mla_q_lora_g,
           mla_kv_lora_g, mla_w_q_up, mla_w_kv_up, mla_nope_g, mla_rope_g, gqa_qk_g, s5_lambda_re, s5_lambda_im,
           s5_log_step, s5_b_re, s5_b_im, s5_c_re, s5_c_im, s5_d, s5_w_glu, w_branch, w_out, moe_w_router,
           moe_b_router, moe_w_gate, moe_b_gate, moe_w_up, moe_b_up, moe_w_down, moe_b_down):
    L, D = DEPTH, D_MODEL
    tm = 512
    ctx_blocks_1k = N_CTX // DEC_SEQ

    x = jnp.concatenate([x_prompt.reshape(N_CTX, D), x_sample.reshape(N_LAT, D)], axis=0).astype(F32)
    cond = jnp.zeros((MOD_ROWS, D), F32).at[0].set(c_ctx.astype(F32)).at[1:1 + DEC_BATCH].set(c.astype(F32))
    mod4 = _ada_modulation(cond, w_ada, b_ada).reshape(L, MOD_ROWS, 1, 6 * D)
    norm_g4 = norm_g.astype(F32).reshape(L, 2, 1, D)

    mla_tabs, mla_half = _rope_tables(MLA_ROPE)
    gqa_tabs, gqa_half = _rope_tables(HEAD_DIM)
    na_bias = _na_bias(na_rpb)
    lam_tiles, s5_wb, s5_wc = _s5_params(s5_lambda_re, s5_lambda_im, s5_log_step, s5_b_re, s5_b_im, s5_c_re, s5_c_im)
    h0_lat = _s5_state_tiles(state_s5_re.reshape(-1, 2, SSM_GROUPS, SSM_STATE),
                             state_s5_im.reshape(-1, 2, SSM_GROUPS, SSM_STATE)).reshape(DEC_BATCH, L, 128, LANE)
    h0_ctx = jnp.zeros((BATCH, 128, LANE), F32)
    s5_d3 = s5_d.astype(F32).reshape(L, 1, -1)

    wq = mla_w_q_up.reshape(L, MLA_Q_LORA, MLA_HEADS, MLA_NOPE + MLA_ROPE)
    wq_perm = jnp.concatenate([
        wq[..., :MLA_NOPE].reshape(L, MLA_Q_LORA, MLA_HEADS * MLA_NOPE),
        jnp.pad(wq[..., MLA_NOPE:], ((0, 0), (0, 0), (0, 0), (0, HEAD_DIM - MLA_ROPE))).reshape(L, MLA_Q_LORA, -1),
    ], axis=-1)
    wkv_perm = mla_w_kv_up.reshape(L, MLA_KV_LORA, MLA_HEADS, 2, HEAD_DIM).transpose(0, 1, 3, 2, 4).reshape(
        L, MLA_KV_LORA, 2 * MLA_HEADS * HEAD_DIM)
    rope_g_pad = jnp.pad(mla_rope_g.astype(F32), ((0, 0), (0, 0), (0, HEAD_DIM - MLA_ROPE)))
    cache_krope_pad = jnp.pad(cache_mla_krope, ((0, 0), (0, 0), (0, 0), (0, HEAD_DIM - MLA_ROPE)))
    cache_ckv_rows = cache_mla_ckv.reshape(DEC_BATCH * L * PAST_LEN, MLA_KV_LORA)
    c_na_k = cache_na_k.reshape(DEC_BATCH, L, PAST_LEN, NA_HEADS * HEAD_DIM)
    c_na_v = cache_na_v.reshape(DEC_BATCH, L, PAST_LEN, NA_HEADS * HEAD_DIM)
    c_gqa_k = cache_gqa_k.reshape(DEC_BATCH, L, PAST_LEN, GQA_KV_HEADS * HEAD_DIM)
    c_gqa_v = cache_gqa_v.reshape(DEC_BATCH, L, PAST_LEN, GQA_KV_HEADS * HEAD_DIM)

    wr = jnp.pad(moe_w_router.astype(F32), ((0, 0), (0, 0), (0, LANE - N_EXPERTS)))
    wr_hi = wr.astype(BF16)
    wr_lo = (wr - wr_hi.astype(F32)).astype(BF16)
    br = jnp.pad(moe_b_router.astype(F32), ((0, 0), (0, LANE - N_EXPERTS))).reshape(L, 1, LANE)
    b_gate4 = moe_b_gate.astype(F32).reshape(L, N_EXPERTS, 1, D_FF)
    b_up4 = moe_b_up.astype(F32).reshape(L, N_EXPERTS, 1, D_FF)
    b_down4 = moe_b_down.astype(F32).reshape(L, N_EXPERTS, 1, D)

    sm_scale = HEAD_DIM ** -0.5
    mla_scale = (MLA_NOPE + MLA_ROPE) ** -0.5
    g1 = lambda v: v.astype(F32).reshape(1, -1)

    def ctx_rows(width, cb):
        return (SEQ, width), lambda b, qi: (b, cb)

    def lat_rows(width, cb):
        return (DEC_SEQ, width), lambda b, qi: (ctx_blocks_1k + b, cb)

    def cache_rows(width, layer):
        return (None, None, PAST_LEN, width), lambda b, qi: (b, layer, 0, 0)

    def attend(q_parts, k_parts, v, ctx_src, *, k_heads, v_heads, bias=None, name):
        def src(rows):
            return ([(a, *rows(w, 0)) for a, w in k_parts], (v[0], *rows(v[1], v[2])))
        common = dict(heads=NA_HEADS, k_heads=k_heads, v_heads=v_heads)
        o_ctx = _attention(q_parts, [src(ctx_rows)], n_batch=BATCH, seq=SEQ, tq=SEQ, q_row0=0, name=name + "_ctx", **common)
        o_lat = _attention(q_parts, [src(lat_rows), ctx_src], n_batch=DEC_BATCH, seq=DEC_SEQ, tq=256, q_row0=N_CTX,
                           bias=bias, name=name + "_lat", **common)
        return jnp.concatenate([o_ctx, o_lat], axis=0)

    new = {k: [] for k in ("na_k", "na_v", "ckv", "krope", "gqa_k", "gqa_v", "s5")}
    for l in range(L):
        h = _norm1(x, norm_g4, mod4, l)
        win = functools.partial(_matmul, h, w_in, w_lead=(l,), tm=tm)
        z1 = win(col0=0, ncols=COL_BKR + 512, tn=512, out_dtype=F32, name="w_in_a")
        z2 = win(col0=COL_C, ncols=COL_GATE - COL_C, tn=512, out_dtype=F32, name="w_in_c")
        zg = _matmul(h, w_in, w_lead=(l,), tm=1024, col0=COL_GATE, ncols=N_BRANCHES * D, tn=512, out_dtype=BF16,
                     name="w_in_gate")

        qa = _headnorm(z1, g1(na_qk_g[l, 0]) * sm_scale, col_block=0, heads=NA_HEADS, name="na_q")
        ka, ka32 = _headnorm(z1, g1(na_qk_g[l, 1]), col_block=1, heads=NA_HEADS, f32_width=HEAD_DIM, name="na_k")
        o_a = attend([(qa, 0)], [(ka, 1024)], (z1, 1024, 2),
                     ([(c_na_k, *cache_rows(1024, l))], (c_na_v, *cache_rows(1024, l))),
                     k_heads=(NA_HEADS,), v_heads=NA_HEADS,
                     bias=(na_bias, (None, NA_HEADS, 256, DEC_SEQ), lambda b, qi, l=l: (l, 0, qi, 0)), name="na")
        new["na_k"].append(ka32[:N_CTX])
        new["na_v"].append(z1[:N_CTX, 2048:3072])

        ql = _headnorm(z1, g1(mla_q_lora_g[l]), col_block=3, heads=1, dh=MLA_Q_LORA, name="mla_ql")
        mq = _matmul(ql, wq_perm, w_lead=(l,), tm=tm, tn=512, out_dtype=F32, name="mla_q_up")
        mq_nope = _headnorm(mq, g1(mla_nope_g[l, 0]) * mla_scale, col_block=0, heads=MLA_HEADS, name="mla_q_nope")
        mq_rope = _headnorm(mq, g1(rope_g_pad[l, 0]) * mla_scale, col_block=1, heads=MLA_HEADS, dh_eff=MLA_ROPE,
                            rope=mla_tabs, rope_half=mla_half, name="mla_q_rope")
        ckv, ckv32 = _headnorm(z1, g1(mla_kv_lora_g[l]), col_block=COL_BKR // MLA_KV_LORA - 1, heads=1, dh=MLA_KV_LORA,
                               f32_width=MLA_KV_LORA, name="mla_ckv")
        kr, kr32 = _headnorm(z1, g1(rope_g_pad[l, 1]), col_block=COL_BKR // LANE, heads=1, dh_eff=MLA_ROPE, rope=mla_tabs,
                             rope_half=mla_half, f32_width=MLA_ROPE, name="mla_krope")
        kv = _matmul(ckv, wkv_perm, w_lead=(l,), tm=tm, tn=512, out_dtype=BF16, name="mla_kv_up")
        k_nope = _headnorm(kv, g1(mla_nope_g[l, 1]), col_block=0, heads=MLA_HEADS, name="mla_k_nope")
        kvc = _matmul(cache_ckv_rows, wkv_perm, w_lead=(l,), m_rows=DEC_BATCH * PAST_LEN, tm=PAST_LEN, tn=512,
                      a_index_map=lambda n, m, l=l: (m * L + l, 0), out_dtype=BF16, name="mla_kv_up_cache")
        kc_nope = _headnorm(kvc, g1(mla_nope_g[l, 1]), col_block=0, heads=MLA_HEADS, name="mla_kc_nope")
        ctx_src = ([(kc_nope, (PAST_LEN, 1024), lambda b, qi: (b, 0)), (cache_krope_pad, *cache_rows(HEAD_DIM, l))],
                   (kvc, (PAST_LEN, 1024), lambda b, qi: (b, 1)))
        o_b = attend([(mq_nope, 0), (mq_rope, 0)], [(k_nope, 1024), (kr, HEAD_DIM)], (kv, 1024, 1), ctx_src,
                     k_heads=(MLA_HEADS, 1), v_heads=MLA_HEADS, name="mla")
        new["ckv"].append(ckv32[:N_CTX])
        new["krope"].append(kr32[:N_CTX])

        gq = _headnorm(z2, g1(gqa_qk_g[l, 0]) * sm_scale, col_block=0, heads=GQA_Q_HEADS, rope=gqa_tabs,
                       rope_half=gqa_half, name="gqa_q")
        gk, gk32 = _headnorm(z2, g1(gqa_qk_g[l, 1]), col_block=4, heads=GQA_KV_HEADS, rope=gqa_tabs, rope_half=gqa_half,
                             f32_width=HEAD_DIM, name="gqa_k")
        o_c = attend([(gq, 0)], [(gk, 256)], (z2, 256, 5),
                     ([(c_gqa_k, *cache_rows(256, l))], (c_gqa_v, *cache_rows(256, l))),
                     k_heads=(GQA_KV_HEADS,), v_heads=GQA_KV_HEADS, name="gqa")
        new["gqa_k"].append(gk32[:N_CTX])
        new["gqa_v"].append(z2[:N_CTX, 1280:1536])

        u_cb = 1536 // 256
        bu3 = _matmul(z2, s5_wb, w_lead=(l,), ncols=2 * S5_GC * S5_CHUNK_COLS, tm=tm, tn=S5_CHUNK_COLS, out_dtype=F32,
                      k=256, a_index_map=lambda n, m: (m, u_cb + n % S5_GC), out_tiled=True, name="s5_bu")
        hf_c, hb_c, fin_c = _s5_scan(bu3, lam_tiles, h0_ctx, layer=l, n_batch=BATCH, seq=SEQ, row0=0)
        hf_l, hb_l, _ = _s5_scan(bu3, lam_tiles, h0_lat[:, l], layer=l, n_batch=DEC_BATCH, seq=DEC_SEQ, row0=N_CTX)
        new["s5"].append(fin_c)

        def s5_out(hf, hb, row_block0, n_rows):
            chunk = (tm, S5_CHUNK_COLS // LANE, LANE)
            return _multi_dot(
                [(hf, chunk, lambda n, m: (m, n, 0)), (hb, chunk, lambda n, m: (m, n, 0))],
                s5_wc, pl.BlockSpec((None, 2, None, S5_CHUNK_COLS, 256), lambda n, m: (l, 0, n, 0, 0)),
                m_rows=n_rows, ncols=BRANCH_WIDTH, tm=tm, tn=256, out_dtype=BF16,
                extra=[(z2, (tm, 256), lambda n, m: (row_block0 + m, u_cb + n)),
                       (s5_d3, (None, 1, 256), lambda n, m: (l, 0, n))],
                epilogue=lambda acc, u, dvec: jax.nn.gelu(acc + dvec * u), name="s5_out")

        yy = jnp.concatenate([s5_out(hf_c, hb_c, 0, N_CTX), s5_out(hf_l, hb_l, N_CTX // tm, N_LAT)], axis=0)
        o_d = _matmul(yy, s5_w_glu, w_lead=(l,), tm=tm, tn=512, out_dtype=BF16,
                      extra=[(yy, (tm, 512), lambda n, m: (m, n))],
                      epilogue=lambda acc, y: y.astype(F32) * jax.nn.sigmoid(acc), name="s5_glu")

        gate_blocks = D // 512
        merged = _multi_dot(
            [(o, (tm, BRANCH_WIDTH), lambda n, m: (m, 0)) for o in (o_a, o_b, o_c, o_d)],
            w_branch, pl.BlockSpec((None, N_BRANCHES, BRANCH_WIDTH, 512), lambda n, m: (l, 0, 0, n)),
            m_rows=N_TOK, ncols=D, tm=tm, tn=512, out_dtype=BF16,
            gates=[(zg, (tm, 512), lambda n, m, i=i: (m, i * gate_blocks + n)) for i in range(N_BRANCHES)],
            name="branch_merge")
        x = _matmul(merged, w_out, w_lead=(l,), tm=tm, tn=512, out_dtype=F32,
                    extra=[(x, (tm, 512), lambda n, m: (m, n)),
                           (mod4, (None, None, 1, 512), lambda n, m: (l, _mod_row(m * tm), 0, 2 * gate_blocks + n))],
                    epilogue=lambda acc, xv, g: xv + g * acc, name="w_out")

        hp, e_sel, p_sel = _norm2_router(x, norm_g4, mod4, wr_hi, wr_lo, br, l)
        slot, slot_tok, tile_e, n_valid = _moe_routing(e_sel)
        xs = _moe_gather(slot_tok, n_valid, hp)
        ys = _moe_experts(tile_e, n_valid, xs, moe_w_gate, b_gate4, moe_w_up, b_up4, moe_w_down, b_down4, l)
        x = _moe_combine(slot, ys, x, p_sel, mod4, l)

    y_prompt = x[:N_CTX].reshape(BATCH, SEQ, D).astype(x_prompt.dtype)
    y_sample = x[N_CTX:].reshape(DEC_BATCH, DEC_SEQ, D).astype(x_sample.dtype)
    stack = lambda key, shape: jnp.stack([t.reshape((BATCH, SEQ) + shape) for t in new[key]], axis=1)
    fin = jnp.stack(new["s5"], axis=1).reshape(BATCH * L, 2 * S5_HALF_ROWS, LANE)
    s5_re, s5_im = _s5_split_tiles(fin)
    s5_shape = (BATCH, L, 2, SSM_GROUPS, SSM_STATE)
    return (y_prompt, y_sample,
            stack("na_k", (NA_HEADS, HEAD_DIM)), stack("na_v", (NA_HEADS, HEAD_DIM)),
            stack("ckv", (MLA_KV_LORA,)), stack("krope", (MLA_ROPE,)),
            stack("gqa_k", (GQA_KV_HEADS, HEAD_DIM)), stack("gqa_v", (GQA_KV_HEADS, HEAD_DIM)),
            s5_re.reshape(s5_shape), s5_im.reshape(s5_shape))
```

```python
import functools
import math

import jax
import jax.numpy as jnp
import numpy as np
from jax import lax
from jax.experimental import pallas as pl
from jax.experimental.pallas import tpu as pltpu

F32 = jnp.float32
BF16 = jnp.bfloat16
U32 = jnp.uint32
I32 = jnp.int32

D_MODEL = 4096
BATCH = 32
SEQ = 256
DEPTH = 4
DEC_BATCH = 4
DEC_SEQ = 1024
PAST_LEN = 512
GRID_W = 64
HEAD_DIM = 128
N_BRANCHES = 4
BRANCH_WIDTH = D_MODEL // N_BRANCHES
ROPE_THETA = 10000.0
NEG_INF = -1e30
EPS = 1e-6
NA_HEADS = 8
NA_WIN_H = 8
NA_WIN_W = 16
MLA_HEADS = 8
MLA_NOPE = 128
MLA_ROPE = 64
MLA_Q_LORA = 1024
MLA_KV_LORA = 512
GQA_Q_HEADS = 8
GQA_KV_HEADS = 2
SSM_GROUP_CH = 16
SSM_GROUPS = 64
SSM_STATE = 64
N_EXPERTS = 32
TOP_K = 4
D_FF = 1024
SWIGLU_LIMIT = 7.0
SWIGLU_ALPHA = 1.702

N_CTX = BATCH * SEQ
N_LAT = DEC_BATCH * DEC_SEQ
N_TOK = N_CTX + N_LAT
MOD_ROWS = 8

COL_BKR = 3072 + MLA_Q_LORA + MLA_KV_LORA
COL_C = COL_BKR + MLA_ROPE
COL_GATE = COL_C + 1024 + 256 + 256 + 1024
IN_WIDTH = COL_GATE + N_BRANCHES * D_MODEL

LANE = 128
VMEM_LIMIT = 52 * 1024 * 1024
MOE_UP_VMEM_LIMIT = 57 * 1024 * 1024

S5_GC = 4
S5_CHUNK_COLS = 2048

MOE_TM = 512
MOE_TILES = N_TOK * TOP_K // MOE_TM + N_EXPERTS
MOE_SLOTS = MOE_TILES * MOE_TM
DMA_UNROLL = 8


def _cparams(n_axes, vmem=VMEM_LIMIT):
    return pltpu.CompilerParams(dimension_semantics=("arbitrary",) * n_axes, vmem_limit_bytes=vmem)


def _mod_row(row_start):
    return jnp.where(row_start < N_CTX, 0, 1 + (row_start - N_CTX) // DEC_SEQ)


def _cast_weight_tile(dst_ref, w_ref, w_next_ref, lane_shift, rows_per_step=256):
    k = w_ref.shape[0]
    rows = min(rows_per_step, k)

    def body(i, _):
        r = pl.multiple_of(i * rows, rows)
        w = w_ref[pl.ds(r, rows), :]
        if lane_shift:
            w = jnp.concatenate([w[:, lane_shift:], w_next_ref[pl.ds(r, rows), :lane_shift]], axis=1)
        dst_ref[pl.ds(r, rows), :] = w.astype(BF16)
        return 0

    lax.fori_loop(0, k // rows, body, 0)


def _store_lane_tiles(o_ref, val):
    for r in range(o_ref.shape[1]):
        o_ref[:, r, :] = val[:, r * LANE:(r + 1) * LANE].astype(o_ref.dtype)


def _load_lane_tiles(x_ref):
    return jnp.concatenate([x_ref[:, r, :] for r in range(x_ref.shape[1])], axis=1)


def _mm_kernel(*refs, lane_shift, n_extra, epilogue):
    a_ref, w_ref = refs[0], refs[1]
    pos = 2
    w_next_ref = None
    if lane_shift:
        w_next_ref = refs[pos]
        pos += 1
    extra = refs[pos:pos + n_extra]
    o_ref = refs[pos + n_extra]
    wb_ref = refs[pos + n_extra + 1]

    @pl.when(pl.program_id(1) == 0)
    def _():
        _cast_weight_tile(wb_ref, w_ref, w_next_ref, lane_shift)

    acc = jnp.dot(a_ref[...].astype(BF16), wb_ref[...], preferred_element_type=F32)
    res = epilogue(acc, *[e[...] for e in extra])
    if len(o_ref.shape) == 3:
        _store_lane_tiles(o_ref, res)
    else:
        o_ref[...] = res.astype(o_ref.dtype)


def _matmul(a, w, *, w_lead=(), col0=0, ncols=None, tm, tn, out_dtype, a_index_map=None, m_rows=None, k=None,
            extra=(), epilogue=None, out_tiled=False, name="matmul"):
    m_rows = m_rows or a.shape[0]
    a_index_map = a_index_map or (lambda n, m: (m, 0))
    k = k or w.shape[-2]
    ncols = ncols or w.shape[-1]
    lane_shift = col0 % LANE
    assert lane_shift in (0, 64) and m_rows % tm == 0 and ncols % tn == 0 and (col0 - lane_shift) % tn == 0
    nb0 = (col0 - lane_shift) // tn
    lead = tuple(w_lead)
    nlead = (None,) * len(lead)
    in_specs = [
        pl.BlockSpec((tm, k), a_index_map),
        pl.BlockSpec(nlead + (k, tn), lambda n, m: lead + (0, nb0 + n)),
    ]
    args = [a, w]
    if lane_shift:
        in_specs.append(pl.BlockSpec(nlead + (k, LANE), lambda n, m: lead + (0, (nb0 + n + 1) * (tn // LANE))))
        args.append(w)
    for arr, bshape, imap in extra:
        in_specs.append(pl.BlockSpec(bshape, imap))
        args.append(arr)
    epi = epilogue or (lambda acc: acc)
    if out_tiled:
        out_spec = pl.BlockSpec((tm, tn // LANE, LANE), lambda n, m: (m, n, 0))
        out_shape = jax.ShapeDtypeStruct((m_rows, ncols // LANE, LANE), out_dtype)
    else:
        out_spec = pl.BlockSpec((tm, tn), lambda n, m: (m, n))
        out_shape = jax.ShapeDtypeStruct((m_rows, ncols), out_dtype)
    return pl.pallas_call(
        functools.partial(_mm_kernel, lane_shift=lane_shift, n_extra=len(extra), epilogue=epi),
        grid=(ncols // tn, m_rows // tm),
        in_specs=in_specs,
        out_specs=out_spec,
        out_shape=out_shape,
        scratch_shapes=[pltpu.VMEM((k, tn), BF16)],
        compiler_params=_cparams(2),
        name=name,
    )(*args)


def _ada_kernel(c_ref, w_ref, b_ref, o_ref, wb_ref):
    _cast_weight_tile(wb_ref, w_ref, None, 0)
    c = c_ref[...]
    s = (c * jax.nn.sigmoid(c)).astype(BF16)
    o_ref[...] = jnp.dot(s, wb_ref[...], preferred_element_type=F32) + b_ref[...]


def _ada_modulation(cond, w_ada, b_ada, tn=512):
    depth, d, n = w_ada.shape
    rows = cond.shape[0]
    return pl.pallas_call(
        _ada_kernel,
        grid=(depth, n // tn),
        in_specs=[
            pl.BlockSpec((rows, d), lambda l, j: (0, 0)),
            pl.BlockSpec((None, d, tn), lambda l, j: (l, 0, j)),
            pl.BlockSpec((None, 1, tn), lambda l, j: (l, 0, j)),
        ],
        out_specs=pl.BlockSpec((None, rows, tn), lambda l, j: (l, 0, j)),
        out_shape=jax.ShapeDtypeStruct((depth, rows, n), F32),
        scratch_shapes=[pltpu.VMEM((d, tn), BF16)],
        compiler_params=_cparams(2),
        name="ada_modulation",
    )(cond, w_ada, b_ada.reshape(depth, 1, n))


def _pack_bf16_pair(lo, hi):
    lo_bits = lax.bitcast_convert_type(lo.astype(BF16).astype(F32), U32) >> 16
    hi_bits = lax.bitcast_convert_type(hi.astype(BF16).astype(F32), U32) & jnp.uint32(0xFFFF0000)
    return lo_bits | hi_bits


def _unpack_bf16_pair(word):
    lo = lax.bitcast_convert_type(word << 16, F32)
    hi = lax.bitcast_convert_type(word & jnp.uint32(0xFFFF0000), F32)
    return lo, hi


def _modulated_norm(x, g, scale, shift):
    y = x * lax.rsqrt(jnp.mean(x * x, axis=-1, keepdims=True) + EPS)
    return (y * g) * (1.0 + scale) + shift


def _norm_kernel(x_ref, g_ref, shift_ref, scale_ref, o_ref):
    o_ref[...] = _modulated_norm(x_ref[...], g_ref[...], scale_ref[...], shift_ref[...]).astype(o_ref.dtype)


def _split_hi_lo(v):
    hi = v.astype(BF16)
    return hi, (v - hi.astype(F32)).astype(BF16)


def _norm_router_kernel(x_ref, g_ref, shift_ref, scale_ref, wr_hi_ref, wr_lo_ref, br_ref,
                        hp_ref, e_ref, p_ref):
    h = _modulated_norm(x_ref[...], g_ref[...], scale_ref[...], shift_ref[...])
    half = h.shape[1] // 2
    hp_ref[...] = _pack_bf16_pair(h[:, :half], h[:, half:])
    h_hi, h_lo = _split_hi_lo(h)
    logits = (jnp.dot(h_hi, wr_hi_ref[...], preferred_element_type=F32)
              + jnp.dot(h_hi, wr_lo_ref[...], preferred_element_type=F32)
              + jnp.dot(h_lo, wr_hi_ref[...], preferred_element_type=F32)) + br_ref[...]
    lane = lax.broadcasted_iota(I32, logits.shape, 1)
    logits = jnp.where(lane < N_EXPERTS, logits, -jnp.inf)
    e_out = jnp.zeros(logits.shape, I32)
    p_out = jnp.zeros(logits.shape, F32)
    top = None
    denom = None
    for kk in range(TOP_K):
        m = jnp.max(logits, axis=-1, keepdims=True)
        idx = jnp.min(jnp.where(logits == m, lane, LANE), axis=-1, keepdims=True)
        if kk == 0:
            top = m
        p = jnp.exp(m - top)
        denom = p if kk == 0 else denom + p
        e_out = jnp.where(lane == kk, idx, e_out)
        p_out = jnp.where(lane == kk, p, p_out)
        logits = jnp.where(lane == idx, -jnp.inf, logits)
    e_ref[...] = e_out
    p_ref[...] = p_out / denom


def _norm_specs(tm, d, layer, which):
    return [
        pl.BlockSpec((tm, d), lambda i: (i, 0)),
        pl.BlockSpec((None, None, 1, d), lambda i: (layer, which, 0, 0)),
        pl.BlockSpec((None, None, 1, d), lambda i: (layer, _mod_row(i * tm), 0, 3 * which)),
        pl.BlockSpec((None, None, 1, d), lambda i: (layer, _mod_row(i * tm), 0, 3 * which + 1)),
    ]


def _norm1(x, norm_g4, mod4, layer, tm=256):
    n, d = x.shape
    return pl.pallas_call(
        _norm_kernel,
        grid=(n // tm,),
        in_specs=_norm_specs(tm, d, layer, 0),
        out_specs=pl.BlockSpec((tm, d), lambda i: (i, 0)),
        out_shape=jax.ShapeDtypeStruct((n, d), BF16),
        compiler_params=_cparams(1),
        name="norm1",
    )(x, norm_g4, mod4, mod4)


def _norm2_router(x, norm_g4, mod4, wr_hi, wr_lo, br, layer, tm=256):
    n, d = x.shape
    const = lambda i: (layer, 0, 0)
    return pl.pallas_call(
        _norm_router_kernel,
        grid=(n // tm,),
        in_specs=_norm_specs(tm, d, layer, 1) + [
            pl.BlockSpec((None, d, LANE), const),
            pl.BlockSpec((None, d, LANE), const),
            pl.BlockSpec((None, 1, LANE), const),
        ],
        out_specs=[
            pl.BlockSpec((tm, d // 2), lambda i: (i, 0)),
            pl.BlockSpec((tm, LANE), lambda i: (i, 0)),
            pl.BlockSpec((tm, LANE), lambda i: (i, 0)),
        ],
        out_shape=[
            jax.ShapeDtypeStruct((n, d // 2), U32),
            jax.ShapeDtypeStruct((n, LANE), I32),
            jax.ShapeDtypeStruct((n, LANE), F32),
        ],
        compiler_params=_cparams(1),
        name="norm2_router",
    )(x, norm_g4, mod4, mod4, wr_hi, wr_lo, br)


def _headnorm_kernel(*refs, heads, dh, dh_eff, rope_half, f32_width):
    x_ref, g_ref = refs[0], refs[1]
    pos = 2
    tabs = None
    if rope_half:
        tabs = [r[...] for r in refs[pos:pos + 3]]
        pos += 3
    ob_ref = refs[pos]
    of_ref = refs[pos + 1] if f32_width else None
    g = g_ref[...]
    for h in range(heads):
        x = x_ref[:, h * dh:(h + 1) * dh].astype(F32)
        if dh_eff < dh:
            lane = lax.broadcasted_iota(I32, x.shape, 1)
            x = jnp.where(lane < dh_eff, x, 0.0)
        y = x * lax.rsqrt(jnp.sum(x * x, axis=-1, keepdims=True) * (1.0 / dh_eff) + EPS) * g
        if rope_half:
            c, sa, sb = tabs
            y = y * c + pltpu.roll(y, rope_half, 1) * sa + pltpu.roll(y, dh - rope_half, 1) * sb
        ob_ref[:, h * dh:(h + 1) * dh] = y.astype(ob_ref.dtype)
        if of_ref is not None:
            of_ref[:, h * f32_width:(h + 1) * f32_width] = y[:, :f32_width]


def _headnorm(x, gain, *, col_block, heads, dh=HEAD_DIM, dh_eff=None, rope=None, rope_half=0,
              f32_width=0, tm=256, name="headnorm"):
    n = x.shape[0]
    w = heads * dh
    dh_eff = dh_eff or dh
    in_specs = [pl.BlockSpec((tm, w), lambda i: (i, col_block)), pl.BlockSpec((1, dh), lambda i: (0, 0))]
    args = [x, gain]
    if rope is not None:
        in_specs += [pl.BlockSpec((tm, dh), lambda i: (i, 0))] * 3
        args += list(rope)
    out_specs = [pl.BlockSpec((tm, w), lambda i: (i, 0))]
    out_shape = [jax.ShapeDtypeStruct((n, w), BF16)]
    if f32_width:
        out_specs.append(pl.BlockSpec((tm, heads * f32_width), lambda i: (i, 0)))
        out_shape.append(jax.ShapeDtypeStruct((n, heads * f32_width), F32))
    out = pl.pallas_call(
        functools.partial(_headnorm_kernel, heads=heads, dh=dh, dh_eff=dh_eff,
                          rope_half=rope_half if rope is not None else 0, f32_width=f32_width),
        grid=(n // tm,),
        in_specs=in_specs,
        out_specs=out_specs,
        out_shape=out_shape,
        compiler_params=_cparams(1),
        name=name,
    )(*args)
    return out if f32_width else out[0]


def _attn_kernel(*refs, heads, n_parts, k_heads, v_heads, n_src, has_bias):
    d = HEAD_DIM
    q_refs = refs[:n_parts]
    pos = n_parts
    srcs = []
    for _ in range(n_src):
        srcs.append((refs[pos:pos + n_parts], refs[pos + n_parts]))
        pos += n_parts + 1
    bias_ref = refs[pos] if has_bias else None
    o_ref = refs[-1]
    contract_last = (((1,), (1,)), ((), ()))
    for h in range(heads):
        scores = []
        for si, (k_refs, _) in enumerate(srcs):
            s = None
            for p in range(n_parts):
                kh = h // (heads // k_heads[p])
                q = q_refs[p][:, h * d:(h + 1) * d]
                k = k_refs[p][:, kh * d:(kh + 1) * d].astype(BF16)
                part = lax.dot_general(q, k, contract_last, preferred_element_type=F32)
                s = part if s is None else s + part
            if si == 0 and has_bias:
                s = s + bias_ref[h]
            scores.append(s)
        m = functools.reduce(jnp.maximum, [jnp.max(s, axis=-1, keepdims=True) for s in scores])
        probs = [jnp.exp(s - m) for s in scores]
        denom = functools.reduce(lambda a, b: a + b, [jnp.sum(p, axis=-1, keepdims=True) for p in probs])
        vh = h // (heads // v_heads)
        o = None
        for p, (_, v_ref) in zip(probs, srcs):
            v = v_ref[:, vh * d:(vh + 1) * d].astype(BF16)
            pv = jnp.dot(p.astype(BF16), v, preferred_element_type=F32)
            o = pv if o is None else o + pv
        o_ref[:, h * d:(h + 1) * d] = (o / denom).astype(o_ref.dtype)


def _attention(q_parts, sources, *, n_batch, seq, tq, q_row0, heads, k_heads, v_heads, bias=None, name="attention"):
    w = heads * HEAD_DIM
    nq = seq // tq
    qb0 = q_row0 // tq
    in_specs, args = [], []
    for arr, cb in q_parts:
        in_specs.append(pl.BlockSpec((tq, w), lambda b, qi, cb=cb: (qb0 + b * nq + qi, cb)))
        args.append(arr)
    for k_parts, v in sources:
        for arr, bshape, imap in list(k_parts) + [v]:
            in_specs.append(pl.BlockSpec(bshape, imap))
            args.append(arr)
    if bias is not None:
        in_specs.append(pl.BlockSpec(bias[1], bias[2]))
        args.append(bias[0])
    return pl.pallas_call(
        functools.partial(_attn_kernel, heads=heads, n_parts=len(q_parts), k_heads=k_heads, v_heads=v_heads,
                          n_src=len(sources), has_bias=bias is not None),
        grid=(n_batch, nq),
        in_specs=in_specs,
        out_specs=pl.BlockSpec((tq, w), lambda b, qi: (b * nq + qi, 0)),
        out_shape=jax.ShapeDtypeStruct((n_batch * seq, w), BF16),
        compiler_params=_cparams(2),
        name=name,
    )(*args)


S5_HALF_ROWS = 64


def _s5_step(h, bu, lam):
    h4 = h.reshape(S5_GC, 2, 8, LANE)
    b4 = bu.reshape(S5_GC, 2, 8, LANE)
    a4 = lam.reshape(S5_GC, 2, 8, LANE)
    hr, hi = h4[:, 0], h4[:, 1]
    ar, ai = a4[:, 0], a4[:, 1]
    nr = ar * hr - ai * hi + b4[:, 0]
    ni = ar * hi + ai * hr + b4[:, 1]
    return jnp.stack([nr, ni], axis=1).reshape(S5_HALF_ROWS, LANE)


def _s5_scan_kernel(buf_ref, bub_ref, lam_ref, h0_ref, hf_ref, hb_ref, fin_ref, state_ref, *, tc):
    j = pl.program_id(1)

    @pl.when(j == 0)
    def _():
        state_ref[...] = h0_ref[...]

    lam_f = lam_ref[:S5_HALF_ROWS, :]
    lam_b = lam_ref[S5_HALF_ROWS:, :]

    def body(i, carry):
        hf, hb = carry
        hf = _s5_step(hf, buf_ref[i], lam_f)
        hb = _s5_step(hb, bub_ref[tc - 1 - i], lam_b)
        hf_ref[i] = hf.astype(hf_ref.dtype)
        hb_ref[tc - 1 - i] = hb.astype(hb_ref.dtype)
        return hf, hb

    hf, hb = lax.fori_loop(0, tc, body, (state_ref[:S5_HALF_ROWS, :], state_ref[S5_HALF_ROWS:, :]), unroll=4)
    state_ref[:S5_HALF_ROWS, :] = hf
    state_ref[S5_HALF_ROWS:, :] = hb

    @pl.when(j == pl.num_programs(1) - 1)
    def _():
        fin_ref[...] = state_ref[...]


def _s5_scan(bu3, lam_tiles, h0, *, layer, n_batch, seq, row0, tc=64):
    nt = seq // tc
    rb0 = row0 // tc
    n = n_batch * seq
    return pl.pallas_call(
        functools.partial(_s5_scan_kernel, tc=tc),
        grid=(n_batch, nt),
        in_specs=[
            pl.BlockSpec((tc, S5_HALF_ROWS, LANE), lambda b, j: (rb0 + b * nt + j, 0, 0)),
            pl.BlockSpec((tc, S5_HALF_ROWS, LANE), lambda b, j: (rb0 + b * nt + (nt - 1 - j), 1, 0)),
            pl.BlockSpec((None, 2 * S5_HALF_ROWS, LANE), lambda b, j: (layer, 0, 0)),
            pl.BlockSpec((None, 2 * S5_HALF_ROWS, LANE), lambda b, j: (b, 0, 0)),
        ],
        out_specs=[
            pl.BlockSpec((tc, S5_HALF_ROWS, LANE), lambda b, j: (b * nt + j, 0, 0)),
            pl.BlockSpec((tc, S5_HALF_ROWS, LANE), lambda b, j: (b * nt + (nt - 1 - j), 0, 0)),
            pl.BlockSpec((None, 2 * S5_HALF_ROWS, LANE), lambda b, j: (b, 0, 0)),
        ],
        out_shape=[
            jax.ShapeDtypeStruct((n, S5_HALF_ROWS, LANE), BF16),
            jax.ShapeDtypeStruct((n, S5_HALF_ROWS, LANE), BF16),
            jax.ShapeDtypeStruct((n_batch, 2 * S5_HALF_ROWS, LANE), F32),
        ],
        scratch_shapes=[pltpu.VMEM((2 * S5_HALF_ROWS, LANE), F32)],
        compiler_params=_cparams(2),
        name="s5_scan",
    )(bu3, bu3, lam_tiles, h0)


def _multi_dot_kernel(*refs, n_terms, gated, n_extra, epilogue):
    a_refs = refs[:n_terms]
    w_ref = refs[n_terms]
    pos = n_terms + 1
    g_refs = refs[pos:pos + n_terms] if gated else None
    pos += n_terms if gated else 0
    extra = refs[pos:pos + n_extra]
    o_ref = refs[pos + n_extra]
    wb_ref = refs[pos + n_extra + 1]

    @pl.when(pl.program_id(1) == 0)
    def _():
        for t in range(n_terms):
            _cast_weight_tile(wb_ref.at[t], w_ref.at[t], None, 0)

    acc = None
    for t in range(n_terms):
        a = _load_lane_tiles(a_refs[t]) if len(a_refs[t].shape) == 3 else a_refs[t][...]
        part = jnp.dot(a.astype(BF16), wb_ref[t], preferred_element_type=F32)
        if gated:
            part = part * jax.nn.sigmoid(g_refs[t][...].astype(F32))
        acc = part if acc is None else acc + part
    o_ref[...] = epilogue(acc, *[e[...] for e in extra]).astype(o_ref.dtype)


def _multi_dot(a_terms, w, w_spec, *, m_rows, ncols, tm, tn, out_dtype, gates=None, extra=(), epilogue=None,
               name="multi_dot"):
    n_terms = len(a_terms)
    kdim = math.prod(a_terms[0][1][1:])
    in_specs = [pl.BlockSpec(bs, im) for _, bs, im in a_terms] + [w_spec]
    args = [a for a, _, _ in a_terms] + [w]
    if gates is not None:
        in_specs += [pl.BlockSpec(bs, im) for _, bs, im in gates]
        args += [g for g, _, _ in gates]
    for arr, bshape, imap in extra:
        in_specs.append(pl.BlockSpec(bshape, imap))
        args.append(arr)
    epi = epilogue or (lambda acc: acc)
    return pl.pallas_call(
        functools.partial(_multi_dot_kernel, n_terms=n_terms, gated=gates is not None, n_extra=len(extra), epilogue=epi),
        grid=(ncols // tn, m_rows // tm),
        in_specs=in_specs,
        out_specs=pl.BlockSpec((tm, tn), lambda n, m: (m, n)),
        out_shape=jax.ShapeDtypeStruct((m_rows, ncols), out_dtype),
        scratch_shapes=[pltpu.VMEM((n_terms, kdim, tn), BF16)],
        compiler_params=_cparams(2),
        name=name,
    )(*args)


def _row_gather(idx_of, src_hbm, dst_of, sem, n_rows):
    def copy(r):
        return pltpu.make_async_copy(src_hbm.at[pl.ds(idx_of(r), 1)], dst_of(r), sem)

    def start():
        def body(g, c):
            for j in range(DMA_UNROLL):
                copy(g * DMA_UNROLL + j).start(priority=j % 2)
            return c
        lax.fori_loop(0, n_rows // DMA_UNROLL, body, 0)

    def wait():
        def body(g, c):
            for j in range(DMA_UNROLL):
                copy(g * DMA_UNROLL + j).wait()
            return c
        lax.fori_loop(0, n_rows // DMA_UNROLL, body, 0)

    return start, wait


def _moe_gather_kernel(tok_ref, nv_ref, hp_hbm, o_ref, buf, sem, *, tm):
    t = pl.program_id(0)
    nv = nv_ref[0]

    def tile_ops(tile, slot):
        return _row_gather(lambda r: tok_ref[tile * tm + r], hp_hbm,
                           lambda r: buf.at[slot, pl.ds(r, 1)], sem.at[slot], tm)

    @pl.when((t == 0) & (nv > 0))
    def _():
        tile_ops(0, 0)[0]()

    @pl.when(t + 1 < nv)
    def _():
        tile_ops(t + 1, (t + 1) % 2)[0]()

    @pl.when(t < nv)
    def _():
        slot = t % 2
        tile_ops(t, slot)[1]()
        half = buf.shape[2]
        chunk = 512
        for c0 in range(0, half, chunk):
            lo, hi = _unpack_bf16_pair(buf[slot, :, c0:c0 + chunk])
            o_ref[:, c0:c0 + chunk] = lo.astype(BF16)
            o_ref[:, half + c0:half + c0 + chunk] = hi.astype(BF16)

    @pl.when(t >= nv)
    def _():
        o_ref[...] = jnp.zeros_like(o_ref)


def _moe_gather(slot_tok, n_valid, hp, tm=MOE_TM):
    n_slots = slot_tok.shape[0]
    half = hp.shape[1]
    return pl.pallas_call(
        functools.partial(_moe_gather_kernel, tm=tm),
        grid_spec=pltpu.PrefetchScalarGridSpec(
            num_scalar_prefetch=2,
            grid=(n_slots // tm,),
            in_specs=[pl.BlockSpec(memory_space=pl.ANY)],
            out_specs=pl.BlockSpec((tm, 2 * half), lambda t, tok, nv: (t, 0)),
            scratch_shapes=[pltpu.VMEM((2, tm, half), U32), pltpu.SemaphoreType.DMA((2,))],
        ),
        out_shape=jax.ShapeDtypeStruct((n_slots, 2 * half), BF16),
        compiler_params=_cparams(1),
        name="moe_gather",
    )(slot_tok, n_valid, hp)


def _moe_tile_flags(te_ref, nv_ref, t):
    valid = t < nv_ref[0]
    new_expert = (t == 0) | (te_ref[t] != te_ref[jnp.maximum(t - 1, 0)])
    return valid, valid & new_expert


def _moe_up_kernel(te_ref, nv_ref, xs_ref, wg_ref, wu_ref, bg_ref, bu_ref, o_ref, wb_ref):
    valid, recast = _moe_tile_flags(te_ref, nv_ref, pl.program_id(1))

    @pl.when(recast)
    def _():
        _cast_weight_tile(wb_ref.at[0], wg_ref, None, 0)
        _cast_weight_tile(wb_ref.at[1], wu_ref, None, 0)

    @pl.when(valid)
    def _():
        x = xs_ref[...]
        gate = jnp.minimum(jnp.dot(x, wb_ref[0], preferred_element_type=F32) + bg_ref[...], SWIGLU_LIMIT)
        up = jnp.clip(jnp.dot(x, wb_ref[1], preferred_element_type=F32) + bu_ref[...], -SWIGLU_LIMIT, SWIGLU_LIMIT)
        o_ref[...] = (gate * jax.nn.sigmoid(SWIGLU_ALPHA * gate) * (up + 1.0)).astype(o_ref.dtype)

    @pl.when(jnp.logical_not(valid))
    def _():
        o_ref[...] = jnp.zeros_like(o_ref)


def _moe_down_kernel(te_ref, nv_ref, act_ref, wd_ref, bd_ref, o_ref, wb_ref):
    valid, recast = _moe_tile_flags(te_ref, nv_ref, pl.program_id(1))

    @pl.when(recast)
    def _():
        _cast_weight_tile(wb_ref, wd_ref, None, 0)

    @pl.when(valid)
    def _():
        y = jnp.dot(act_ref[...], wb_ref[...], preferred_element_type=F32) + bd_ref[...]
        half = y.shape[1] // 2
        o_ref[...] = _pack_bf16_pair(y[:, :half], y[:, half:])

    @pl.when(jnp.logical_not(valid))
    def _():
        o_ref[...] = jnp.zeros_like(o_ref)


def _moe_experts(tile_e, n_valid, xs, w_gate, b_gate, w_up, b_up, w_down, b_down, layer, tm=MOE_TM, tf=512, tn=2048):
    n_slots, d = xs.shape
    d_ff = w_gate.shape[-1]
    n_tiles = n_slots // tm
    row = lambda t, nv: jnp.minimum(t, jnp.maximum(nv[0] - 1, 0))
    act = pl.pallas_call(
        _moe_up_kernel,
        grid_spec=pltpu.PrefetchScalarGridSpec(
            num_scalar_prefetch=2,
            grid=(d_ff // tf, n_tiles),
            in_specs=[
                pl.BlockSpec((tm, d), lambda f, t, te, nv: (row(t, nv), 0)),
                pl.BlockSpec((None, None, d, tf), lambda f, t, te, nv: (layer, te[t], 0, f)),
                pl.BlockSpec((None, None, d, tf), lambda f, t, te, nv: (layer, te[t], 0, f)),
                pl.BlockSpec((None, None, 1, tf), lambda f, t, te, nv: (layer, te[t], 0, f)),
                pl.BlockSpec((None, None, 1, tf), lambda f, t, te, nv: (layer, te[t], 0, f)),
            ],
            out_specs=pl.BlockSpec((tm, tf), lambda f, t, te, nv: (t, f)),
            scratch_shapes=[pltpu.VMEM((2, d, tf), BF16)],
        ),
        out_shape=jax.ShapeDtypeStruct((n_slots, d_ff), BF16),
        compiler_params=_cparams(2, MOE_UP_VMEM_LIMIT),
        name="moe_up",
    )(tile_e, n_valid, xs, w_gate, w_up, b_gate, b_up)
    return pl.pallas_call(
        _moe_down_kernel,
        grid_spec=pltpu.PrefetchScalarGridSpec(
            num_scalar_prefetch=2,
            grid=(d // tn, n_tiles),
            in_specs=[
                pl.BlockSpec((tm, d_ff), lambda n, t, te, nv: (row(t, nv), 0)),
                pl.BlockSpec((None, None, d_ff, tn), lambda n, t, te, nv: (layer, te[t], 0, n)),
                pl.BlockSpec((None, None, 1, tn), lambda n, t, te, nv: (layer, te[t], 0, n)),
            ],
            out_specs=pl.BlockSpec((tm, tn // 2), lambda n, t, te, nv: (t, n)),
            scratch_shapes=[pltpu.VMEM((d_ff, tn), BF16)],
        ),
        out_shape=jax.ShapeDtypeStruct((n_slots, d // 2), U32),
        compiler_params=_cparams(2),
        name="moe_down",
    )(tile_e, n_valid, act, w_down, b_down)


def _moe_combine_kernel(slot_ref, ys_hbm, x_ref, p_ref, gate_ref, o_ref, buf, sem, *, tt, tn):
    i = pl.program_id(0)
    n_steps = pl.num_programs(0)
    rows = tt * TOP_K

    def tile_ops(tile, slot):
        return _row_gather(lambda r: slot_ref[tile * rows + r], ys_hbm,
                           lambda r: buf.at[slot, r & (TOP_K - 1), pl.ds(lax.shift_right_logical(r, TOP_K.bit_length() - 1), 1)],
                           sem.at[slot], rows)

    @pl.when(i == 0)
    def _():
        tile_ops(0, 0)[0]()

    @pl.when(i + 1 < n_steps)
    def _():
        tile_ops(i + 1, (i + 1) % 2)[0]()

    slot = i % 2
    tile_ops(i, slot)[1]()
    p = p_ref[...]
    pk = [p[:, kk:kk + 1] for kk in range(TOP_K)]
    hw = tn // 2
    chunk = 512
    for w0 in range(0, x_ref.shape[1] // 2, chunk):
        lo = hi = None
        for kk in range(TOP_K):
            w_lo, w_hi = _unpack_bf16_pair(buf[slot, kk, :, w0:w0 + chunk])
            lo = w_lo * pk[kk] if lo is None else lo + w_lo * pk[kk]
            hi = w_hi * pk[kk] if hi is None else hi + w_hi * pk[kk]
        c_lo = (w0 // hw) * tn + w0 % hw
        c_hi = c_lo + hw
        o_ref[:, c_lo:c_lo + chunk] = x_ref[:, c_lo:c_lo + chunk] + gate_ref[:, c_lo:c_lo + chunk] * lo
        o_ref[:, c_hi:c_hi + chunk] = x_ref[:, c_hi:c_hi + chunk] + gate_ref[:, c_hi:c_hi + chunk] * hi


def _moe_combine(slots, ys, x, p_sel, mod4, layer, tt=128, tn=2048):
    n, d = x.shape
    return pl.pallas_call(
        functools.partial(_moe_combine_kernel, tt=tt, tn=tn),
        grid_spec=pltpu.PrefetchScalarGridSpec(
            num_scalar_prefetch=1,
            grid=(n // tt,),
            in_specs=[
                pl.BlockSpec(memory_space=pl.ANY),
                pl.BlockSpec((tt, d), lambda i, s: (i, 0)),
                pl.BlockSpec((tt, LANE), lambda i, s: (i, 0)),
                pl.BlockSpec((None, None, 1, d), lambda i, s: (layer, _mod_row(i * tt), 0, 5)),
            ],
            out_specs=pl.BlockSpec((tt, d), lambda i, s: (i, 0)),
            scratch_shapes=[pltpu.VMEM((2, TOP_K, tt, d // 2), U32), pltpu.SemaphoreType.DMA((2,))],
        ),
        out_shape=jax.ShapeDtypeStruct((n, d), F32),
        compiler_params=_cparams(1),
        name="moe_combine",
    )(slots, ys, x, p_sel, mod4)


def _rope_tables(rot_dim):
    t = jnp.arange(DEC_SEQ)
    n_freq = rot_dim // 4
    freqs = ROPE_THETA ** (-jnp.arange(n_freq, dtype=F32) / n_freq)
    ang = jnp.concatenate([(t // GRID_W).astype(F32)[:, None] * freqs,
                           (t % GRID_W).astype(F32)[:, None] * freqs], axis=-1)
    cos, sin = jnp.cos(ang), jnp.sin(ang)
    half = rot_dim // 2
    pad = HEAD_DIM - rot_dim
    zeros = jnp.zeros_like(sin)

    def full(lat, ctx_row):
        lat = jnp.pad(lat, ((0, 0), (0, pad)))
        ctx = jnp.broadcast_to(jnp.pad(ctx_row, (0, pad))[None], (N_CTX, HEAD_DIM))
        return jnp.concatenate([ctx, jnp.tile(lat, (DEC_BATCH, 1))], axis=0)

    one = jnp.ones((rot_dim,), F32)
    zero = jnp.zeros((rot_dim,), F32)
    c = full(jnp.concatenate([cos, cos], axis=-1), one)
    sa = full(jnp.concatenate([zeros, sin], axis=-1), zero)
    sb = full(jnp.concatenate([-sin, zeros], axis=-1), zero)
    return (c, sa, sb), half


def _na_bias(na_rpb):
    rows = DEC_SEQ // GRID_W
    wh = min(NA_WIN_H, rows)
    r = np.arange(rows)
    row_lo = np.clip(r - wh // 2, 0, rows - wh)
    row_ok = (r[None, :] >= row_lo[:, None]) & (r[None, :] < row_lo[:, None] + wh)
    d_row = np.clip(r[None, :] - r[:, None] + (NA_WIN_H - 1), 0, 2 * NA_WIN_H - 2)
    col = np.arange(GRID_W)
    col_lo = np.clip(col - NA_WIN_W // 2, 0, GRID_W - NA_WIN_W)
    col_ok = (col[None, :] >= col_lo[:, None]) & (col[None, :] < col_lo[:, None] + NA_WIN_W)
    d_col = np.clip(col[None, :] - col[:, None], -(NA_WIN_W - 1), NA_WIN_W - 1) + (NA_WIN_W - 1)
    sel_row = (d_row[:, :, None] == np.arange(2 * NA_WIN_H - 1)).astype(np.float32)
    sel_col = (d_col[:, :, None] == np.arange(2 * NA_WIN_W - 1)).astype(np.float32)
    rpb = na_rpb.astype(F32)
    t1 = jnp.einsum('lhad,rsa->lhrsd', rpb, sel_row, precision=lax.Precision.HIGHEST)
    t2 = jnp.einsum('lhrsd,qkd->lhrqsk', t1, sel_col, precision=lax.Precision.HIGHEST)
    ok = row_ok[:, None, :, None] & col_ok[None, :, None, :]
    bias = jnp.where(ok[None, None], t2, NEG_INF)
    return bias.reshape(DEPTH, NA_HEADS, DEC_SEQ, DEC_SEQ)


def _s5_params(lam_re, lam_im, log_step, b_re, b_im, c_re, c_im):
    step = jnp.exp(log_step.astype(F32))[..., None]
    lr, li = lam_re.astype(F32), lam_im.astype(F32)
    mag = jnp.exp(lr * step)
    ar, ai = mag * jnp.cos(li * step), mag * jnp.sin(li * step)
    den = lr * lr + li * li
    kr = ((ar - 1.0) * lr + ai * li) / den
    ki = (ai * lr - (ar - 1.0) * li) / den
    bbr = kr[..., None] * b_re - ki[..., None] * b_im
    bbi = kr[..., None] * b_im + ki[..., None] * b_re
    gl = SSM_GROUPS // S5_GC
    eye = jnp.eye(gl, dtype=F32)

    def tiles(re, im):
        v = jnp.stack([re, im], axis=3)
        v = v.reshape(DEPTH, 2, S5_GC, gl, 2, SSM_STATE).transpose(0, 1, 2, 4, 3, 5)
        return v.reshape(DEPTH, 2 * S5_HALF_ROWS, LANE)

    lam_tiles = tiles(ar, ai)
    bb = jnp.stack([bbr, bbi], axis=-1)
    bb = bb.reshape(DEPTH, 2, S5_GC, gl, SSM_STATE, SSM_GROUP_CH, 2)
    wb = bb.transpose(0, 1, 2, 3, 5, 6, 4)[:, :, :, :, :, :, None, :] * eye[None, None, None, :, None, None, :, None]
    wb = wb.reshape(DEPTH, 2 * S5_GC, gl * SSM_GROUP_CH, S5_CHUNK_COLS)
    wb = wb.transpose(0, 2, 1, 3).reshape(DEPTH, gl * SSM_GROUP_CH, 2 * S5_GC * S5_CHUNK_COLS)
    cc = jnp.stack([c_re.astype(F32), -c_im.astype(F32)], axis=2)
    cc = cc.reshape(DEPTH, 2, 2, S5_GC, gl, SSM_GROUP_CH, SSM_STATE)
    wc = cc.transpose(0, 1, 3, 2, 4, 6, 5)[:, :, :, :, :, :, None, :] * eye[None, None, None, None, :, None, :, None]
    wc = wc.reshape(DEPTH, 2, S5_GC, S5_CHUNK_COLS, gl * SSM_GROUP_CH)
    return lam_tiles, wb.astype(BF16), wc.astype(BF16)


def _s5_state_tiles(re, im):
    gl = SSM_GROUPS // S5_GC
    v = jnp.stack([re.astype(F32), im.astype(F32)], axis=3)
    v = v.reshape(-1, 2, S5_GC, gl, 2, SSM_STATE).transpose(0, 1, 2, 4, 3, 5)
    return v.reshape(-1, 2 * S5_HALF_ROWS, LANE)


def _s5_split_tiles(tiles):
    gl = SSM_GROUPS // S5_GC
    v = tiles.reshape(-1, 2, S5_GC, 2, gl, SSM_STATE).transpose(0, 1, 3, 2, 4, 5)
    v = v.reshape(-1, 2, 2, SSM_GROUPS, SSM_STATE)
    return v[:, :, 0], v[:, :, 1]


def _moe_routing(e_sel, tm=MOE_TM):
    flat_e = e_sel[:, :TOP_K].reshape(-1)
    onehot = (flat_e[:, None] == jnp.arange(N_EXPERTS, dtype=I32)[None, :]).astype(I32)
    csum = jnp.cumsum(onehot, axis=0)
    rank = jnp.sum(onehot * (csum - 1), axis=1)
    counts = csum[-1]
    padded = (counts + tm - 1) // tm * tm
    pad_end = jnp.cumsum(padded)
    pad_start = pad_end - padded
    slot = (pad_start[flat_e] + rank).astype(I32)
    n_assign = flat_e.shape[0]
    n_tiles = n_assign // tm + N_EXPERTS
    slot_tok = jnp.zeros((n_tiles * tm,), I32).at[slot].set(jnp.arange(n_assign, dtype=I32) // TOP_K)
    tile_start = jnp.arange(n_tiles, dtype=I32) * tm
    tile_e = jnp.minimum(jnp.sum((pad_end[None, :] <= tile_start[:, None]).astype(I32), axis=1), N_EXPERTS - 1)
    n_valid = (pad_end[-1] // tm).astype(I32).reshape(1)
    return slot, slot_tok, tile_e, n_valid


def kernel(x_prompt, x_sample, cache_na_k, cache_na_v, cache_mla_ckv, cache_mla_krope, cache_gqa_k, cache_gqa_v,
           state_s5_re, state_s5_im, c, c_ctx, norm_g, w_ada, b_ada, w_in, na_qk_g, na_rpb, mla_q_lora_g,
           mla_kv_lora_g, mla_w_q_up, mla_w_kv_up, mla_nope_g, mla_rope_g, gqa_qk_g, s5_lambda_re, s5_lambda_im,
           s5_log_step, s5_b_re, s5_b_im, s5_c_re, s5_c_im, s5_d, s5_w_glu, w_branch, w_out, moe_w_router,
           moe_b_router, moe_w_gate, moe_b_gate, moe_w_up, moe_b_up, moe_w_down, moe_b_down):
    L, D = DEPTH, D_MODEL
    tm = 512
    tm_big = DEC_SEQ
    ctx_blocks_1k = N_CTX // DEC_SEQ

    x = jnp.concatenate([x_prompt.reshape(N_CTX, D), x_sample.reshape(N_LAT, D)], axis=0).astype(F32)
    cond = jnp.zeros((MOD_ROWS, D), F32).at[0].set(c_ctx.astype(F32)).at[1:1 + DEC_BATCH].set(c.astype(F32))
    mod4 = _ada_modulation(cond, w_ada, b_ada).reshape(L, MOD_ROWS, 1, 6 * D)
    norm_g4 = norm_g.astype(F32).reshape(L, 2, 1, D)

    mla_tabs, mla_half = _rope_tables(MLA_ROPE)
    gqa_tabs, gqa_half = _rope_tables(HEAD_DIM)
    na_bias = _na_bias(na_rpb)
    lam_tiles, s5_wb, s5_wc = _s5_params(s5_lambda_re, s5_lambda_im, s5_log_step, s5_b_re, s5_b_im, s5_c_re, s5_c_im)
    h0_lat = _s5_state_tiles(state_s5_re.reshape(-1, 2, SSM_GROUPS, SSM_STATE),
                             state_s5_im.reshape(-1, 2, SSM_GROUPS, SSM_STATE)).reshape(DEC_BATCH, L, 128, LANE)
    h0_ctx = jnp.zeros((BATCH, 128, LANE), F32)
    s5_d3 = s5_d.astype(F32).reshape(L, 1, -1)

    wq = mla_w_q_up.reshape(L, MLA_Q_LORA, MLA_HEADS, MLA_NOPE + MLA_ROPE)
    wq_perm = jnp.concatenate([
        wq[..., :MLA_NOPE].reshape(L, MLA_Q_LORA, MLA_HEADS * MLA_NOPE),
        jnp.pad(wq[..., MLA_NOPE:], ((0, 0), (0, 0), (0, 0), (0, HEAD_DIM - MLA_ROPE))).reshape(L, MLA_Q_LORA, -1),
    ], axis=-1)
    wkv_perm = mla_w_kv_up.reshape(L, MLA_KV_LORA, MLA_HEADS, 2, HEAD_DIM).transpose(0, 1, 3, 2, 4).reshape(
        L, MLA_KV_LORA, 2 * MLA_HEADS * HEAD_DIM)
    rope_g_pad = jnp.pad(mla_rope_g.astype(F32), ((0, 0), (0, 0), (0, HEAD_DIM - MLA_ROPE)))
    cache_krope_pad = jnp.pad(cache_mla_krope, ((0, 0), (0, 0), (0, 0), (0, HEAD_DIM - MLA_ROPE)))
    cache_ckv_rows = cache_mla_ckv.reshape(DEC_BATCH * L * PAST_LEN, MLA_KV_LORA)
    c_na_k = cache_na_k.reshape(DEC_BATCH, L, PAST_LEN, NA_HEADS * HEAD_DIM)
    c_na_v = cache_na_v.reshape(DEC_BATCH, L, PAST_LEN, NA_HEADS * HEAD_DIM)
    c_gqa_k = cache_gqa_k.reshape(DEC_BATCH, L, PAST_LEN, GQA_KV_HEADS * HEAD_DIM)
    c_gqa_v = cache_gqa_v.reshape(DEC_BATCH, L, PAST_LEN, GQA_KV_HEADS * HEAD_DIM)

    wr = jnp.pad(moe_w_router.astype(F32), ((0, 0), (0, 0), (0, LANE - N_EXPERTS)))
    wr_hi = wr.astype(BF16)
    wr_lo = (wr - wr_hi.astype(F32)).astype(BF16)
    br = jnp.pad(moe_b_router.astype(F32), ((0, 0), (0, LANE - N_EXPERTS))).reshape(L, 1, LANE)
    b_gate4 = moe_b_gate.astype(F32).reshape(L, N_EXPERTS, 1, D_FF)
    b_up4 = moe_b_up.astype(F32).reshape(L, N_EXPERTS, 1, D_FF)
    b_down4 = moe_b_down.astype(F32).reshape(L, N_EXPERTS, 1, D)

    sm_scale = HEAD_DIM ** -0.5
    mla_scale = (MLA_NOPE + MLA_ROPE) ** -0.5
    g1 = lambda v: v.astype(F32).reshape(1, -1)

    def ctx_rows(width, cb):
        return (SEQ, width), lambda b, qi: (b, cb)

    def lat_rows(width, cb):
        return (DEC_SEQ, width), lambda b, qi: (ctx_blocks_1k + b, cb)

    def cache_rows(width, layer):
        return (None, None, PAST_LEN, width), lambda b, qi: (b, layer, 0, 0)

    def attend(q_parts, k_parts, v, ctx_src, *, k_heads, v_heads, bias=None, name):
        def src(rows):
            return ([(a, *rows(w, 0)) for a, w in k_parts], (v[0], *rows(v[1], v[2])))
        common = dict(heads=NA_HEADS, k_heads=k_heads, v_heads=v_heads)
        o_ctx = _attention(q_parts, [src(ctx_rows)], n_batch=BATCH, seq=SEQ, tq=SEQ, q_row0=0, name=name + "_ctx", **common)
        o_lat = _attention(q_parts, [src(lat_rows), ctx_src], n_batch=DEC_BATCH, seq=DEC_SEQ, tq=256, q_row0=N_CTX,
                           bias=bias, name=name + "_lat", **common)
        return jnp.concatenate([o_ctx, o_lat], axis=0)

    new = {k: [] for k in ("na_k", "na_v", "ckv", "krope", "gqa_k", "gqa_v", "s5")}
    for l in range(L):
        h = _norm1(x, norm_g4, mod4, l)
        win = functools.partial(_matmul, h, w_in, w_lead=(l,), tm=tm_big)
        z1 = win(col0=0, ncols=COL_BKR + 512, tn=512, out_dtype=F32, name="w_in_a")
        z2 = win(col0=COL_C, ncols=COL_GATE - COL_C, tn=512, out_dtype=F32, name="w_in_c")
        zg = win(col0=COL_GATE, ncols=N_BRANCHES * D, tn=512, out_dtype=BF16, name="w_in_gate")

        qa = _headnorm(z1, g1(na_qk_g[l, 0]) * sm_scale, col_block=0, heads=NA_HEADS, name="na_q")
        ka, ka32 = _headnorm(z1, g1(na_qk_g[l, 1]), col_block=1, heads=NA_HEADS, f32_width=HEAD_DIM, name="na_k")
        o_a = attend([(qa, 0)], [(ka, 1024)], (z1, 1024, 2),
                     ([(c_na_k, *cache_rows(1024, l))], (c_na_v, *cache_rows(1024, l))),
                     k_heads=(NA_HEADS,), v_heads=NA_HEADS,
                     bias=(na_bias, (None, NA_HEADS, 256, DEC_SEQ), lambda b, qi, l=l: (l, 0, qi, 0)), name="na")
        new["na_k"].append(ka32[:N_CTX])
        new["na_v"].append(z1[:N_CTX, 2048:3072])

        ql = _headnorm(z1, g1(mla_q_lora_g[l]), col_block=3, heads=1, dh=MLA_Q_LORA, name="mla_ql")
        mq = _matmul(ql, wq_perm, w_lead=(l,), tm=tm, tn=512, out_dtype=F32, name="mla_q_up")
        mq_nope = _headnorm(mq, g1(mla_nope_g[l, 0]) * mla_scale, col_block=0, heads=MLA_HEADS, name="mla_q_nope")
        mq_rope = _headnorm(mq, g1(rope_g_pad[l, 0]) * mla_scale, col_block=1, heads=MLA_HEADS, dh_eff=MLA_ROPE,
                            rope=mla_tabs, rope_half=mla_half, name="mla_q_rope")
        ckv, ckv32 = _headnorm(z1, g1(mla_kv_lora_g[l]), col_block=COL_BKR // MLA_KV_LORA - 1, heads=1, dh=MLA_KV_LORA,
                               f32_width=MLA_KV_LORA, name="mla_ckv")
        kr, kr32 = _headnorm(z1, g1(rope_g_pad[l, 1]), col_block=COL_BKR // LANE, heads=1, dh_eff=MLA_ROPE, rope=mla_tabs,
                             rope_half=mla_half, f32_width=MLA_ROPE, name="mla_krope")
        kv = _matmul(ckv, wkv_perm, w_lead=(l,), tm=tm, tn=512, out_dtype=BF16, name="mla_kv_up")
        k_nope = _headnorm(kv, g1(mla_nope_g[l, 1]), col_block=0, heads=MLA_HEADS, name="mla_k_nope")
        kvc = _matmul(cache_ckv_rows, wkv_perm, w_lead=(l,), m_rows=DEC_BATCH * PAST_LEN, tm=PAST_LEN, tn=512,
                      a_index_map=lambda n, m, l=l: (m * L + l, 0), out_dtype=BF16, name="mla_kv_up_cache")
        kc_nope = _headnorm(kvc, g1(mla_nope_g[l, 1]), col_block=0, heads=MLA_HEADS, name="mla_kc_nope")
        ctx_src = ([(kc_nope, (PAST_LEN, 1024), lambda b, qi: (b, 0)), (cache_krope_pad, *cache_rows(HEAD_DIM, l))],
                   (kvc, (PAST_LEN, 1024), lambda b, qi: (b, 1)))
        o_b = attend([(mq_nope, 0), (mq_rope, 0)], [(k_nope, 1024), (kr, HEAD_DIM)], (kv, 1024, 1), ctx_src,
                     k_heads=(MLA_HEADS, 1), v_heads=MLA_HEADS, name="mla")
        new["ckv"].append(ckv32[:N_CTX])
        new["krope"].append(kr32[:N_CTX])

        gq = _headnorm(z2, g1(gqa_qk_g[l, 0]) * sm_scale, col_block=0, heads=GQA_Q_HEADS, rope=gqa_tabs,
                       rope_half=gqa_half, name="gqa_q")
        gk, gk32 = _headnorm(z2, g1(gqa_qk_g[l, 1]), col_block=4, heads=GQA_KV_HEADS, rope=gqa_tabs, rope_half=gqa_half,
                             f32_width=HEAD_DIM, name="gqa_k")
        o_c = attend([(gq, 0)], [(gk, 256)], (z2, 256, 5),
                     ([(c_gqa_k, *cache_rows(256, l))], (c_gqa_v, *cache_rows(256, l))),
                     k_heads=(GQA_KV_HEADS,), v_heads=GQA_KV_HEADS, name="gqa")
        new["gqa_k"].append(gk32[:N_CTX])
        new["gqa_v"].append(z2[:N_CTX, 1280:1536])

        u_cb = 1536 // 256
        bu3 = _matmul(z2, s5_wb, w_lead=(l,), ncols=2 * S5_GC * S5_CHUNK_COLS, tm=tm, tn=S5_CHUNK_COLS, out_dtype=F32,
                      k=256, a_index_map=lambda n, m: (m, u_cb + n % S5_GC), out_tiled=True, name="s5_bu")
        hf_c, hb_c, fin_c = _s5_scan(bu3, lam_tiles, h0_ctx, layer=l, n_batch=BATCH, seq=SEQ, row0=0)
        hf_l, hb_l, _ = _s5_scan(bu3, lam_tiles, h0_lat[:, l], layer=l, n_batch=DEC_BATCH, seq=DEC_SEQ, row0=N_CTX)
        new["s5"].append(fin_c)

        def s5_out(hf, hb, row_block0, n_rows):
            chunk = (tm, S5_CHUNK_COLS // LANE, LANE)
            return _multi_dot(
                [(hf, chunk, lambda n, m: (m, n, 0)), (hb, chunk, lambda n, m: (m, n, 0))],
                s5_wc, pl.BlockSpec((None, 2, None, S5_CHUNK_COLS, 256), lambda n, m: (l, 0, n, 0, 0)),
                m_rows=n_rows, ncols=BRANCH_WIDTH, tm=tm, tn=256, out_dtype=BF16,
                extra=[(z2, (tm, 256), lambda n, m: (row_block0 + m, u_cb + n)),
                       (s5_d3, (None, 1, 256), lambda n, m: (l, 0, n))],
                epilogue=lambda acc, u, dvec: jax.nn.gelu(acc + dvec * u), name="s5_out")

        yy = jnp.concatenate([s5_out(hf_c, hb_c, 0, N_CTX), s5_out(hf_l, hb_l, N_CTX // tm, N_LAT)], axis=0)
        o_d = _matmul(yy, s5_w_glu, w_lead=(l,), tm=tm, tn=512, out_dtype=BF16,
                      extra=[(yy, (tm, 512), lambda n, m: (m, n))],
                      epilogue=lambda acc, y: y.astype(F32) * jax.nn.sigmoid(acc), name="s5_glu")

        gate_blocks = D // 512
        merged = _multi_dot(
            [(o, (tm_big, BRANCH_WIDTH), lambda n, m: (m, 0)) for o in (o_a, o_b, o_c, o_d)],
            w_branch, pl.BlockSpec((None, N_BRANCHES, BRANCH_WIDTH, 512), lambda n, m: (l, 0, 0, n)),
            m_rows=N_TOK, ncols=D, tm=tm_big, tn=512, out_dtype=BF16,
            gates=[(zg, (tm_big, 512), lambda n, m, i=i: (m, i * gate_blocks + n)) for i in range(N_BRANCHES)],
            name="branch_merge")
        x = _matmul(merged, w_out, w_lead=(l,), tm=tm_big, tn=512, out_dtype=F32,
                    extra=[(x, (tm_big, 512), lambda n, m: (m, n)),
                           (mod4, (None, None, 1, 512), lambda n, m: (l, _mod_row(m * tm_big), 0, 2 * gate_blocks + n))],
                    epilogue=lambda acc, xv, g: xv + g * acc, name="w_out")

        hp, e_sel, p_sel = _norm2_router(x, norm_g4, mod4, wr_hi, wr_lo, br, l)
        slot, slot_tok, tile_e, n_valid = _moe_routing(e_sel)
        xs = _moe_gather(slot_tok, n_valid, hp)
        ys = _moe_experts(tile_e, n_valid, xs, moe_w_gate, b_gate4, moe_w_up, b_up4, moe_w_down, b_down4, l)
        x = _moe_combine(slot, ys, x, p_sel, mod4, l)

    y_prompt = x[:N_CTX].reshape(BATCH, SEQ, D).astype(x_prompt.dtype)
    y_sample = x[N_CTX:].reshape(DEC_BATCH, DEC_SEQ, D).astype(x_sample.dtype)
    stack = lambda key, shape: jnp.stack([t.reshape((BATCH, SEQ) + shape) for t in new[key]], axis=1)
    fin = jnp.stack(new["s5"], axis=1).reshape(BATCH * L, 2 * S5_HALF_ROWS, LANE)
    s5_re, s5_im = _s5_split_tiles(fin)
    s5_shape = (BATCH, L, 2, SSM_GROUPS, SSM_STATE)
    return (y_prompt, y_sample,
            stack("na_k", (NA_HEADS, HEAD_DIM)), stack("na_v", (NA_HEADS, HEAD_DIM)),
            stack("ckv", (MLA_KV_LORA,)), stack("krope", (MLA_ROPE,)),
            stack("gqa_k", (GQA_KV_HEADS, HEAD_DIM)), stack("gqa_v", (GQA_KV_HEADS, HEAD_DIM)),
            s5_re.reshape(s5_shape), s5_im.reshape(s5_shape))
```
